```python
import math
import jax, jax.numpy as jnp
from jax import lax
import numpy as np

D_MODEL = 1024
BATCH = 8
SEQ = 2048
DEPTH = 4

GRID_W = 64
ATTN_HEADS = 8
HEAD_DIM = 64
ATTN_WIDTH = ATTN_HEADS * HEAD_DIM
WIN_ROWS = 8
WIN_COLS = 16
Q_COLS = 16
BAND_COLS = 2 * WIN_COLS
CONV_WIDTH = D_MODEL - ATTN_WIDTH
CONV_K = 31
IN_WIDTH = 3 * ATTN_WIDTH + 2 * CONV_WIDTH
N_EXPERTS = 16
EC_CAPACITY_FACTOR = 2
EXPERT_FF = 2048
NORM_EPS = 1e-6

kernel_name = "hybrid_natten_conformer_ec_moe_encoder"


def rms_norm(x, g):
    x32 = x.astype(jnp.float32)
    y = x32 * lax.rsqrt(jnp.mean(x32 * x32, axis=-1, keepdims=True) + NORM_EPS)
    return (y * g.astype(jnp.float32)).astype(x.dtype)


def layer_norm(x, g, b):
    x32 = x.astype(jnp.float32)
    mu = jnp.mean(x32, axis=-1, keepdims=True)
    var = jnp.mean(jnp.square(x32 - mu), axis=-1, keepdims=True)
    y = (x32 - mu) * lax.rsqrt(var + NORM_EPS)
    return (y * g.astype(jnp.float32) + b.astype(jnp.float32)).astype(x.dtype)


def neighbourhood_attention(q, k, v, rpb):
    B, S, H, Dh = q.shape
    rows = S // GRID_W
    kh = min(WIN_ROWS, rows)
    n_cb = GRID_W // Q_COLS
    r = jnp.arange(rows)
    row_start = jnp.clip(r - kh // 2, 0, rows - kh)
    row_idx = row_start[:, None] + jnp.arange(kh)[None, :]
    cb = jnp.arange(n_cb)
    band_start = jnp.clip(cb * Q_COLS - WIN_COLS // 2, 0, GRID_W - BAND_COLS)
    col_idx = band_start[:, None] + jnp.arange(BAND_COLS)[None, :]
    qc = cb[:, None] * Q_COLS + jnp.arange(Q_COLS)[None, :]
    win_start = jnp.clip(qc - WIN_COLS // 2, 0, GRID_W - WIN_COLS)
    kc = col_idx[:, None, :]
    col_mask = (kc >= win_start[..., None]) & (kc < win_start[..., None] + WIN_COLS)
    row_off = row_idx - r[:, None] + (WIN_ROWS - 1)
    col_off = jnp.clip(kc - qc[..., None] + (WIN_COLS - 1), 0, 2 * WIN_COLS - 2)
    bias = rpb[:, row_off[:, None, None, :, None], col_off[None, :, :, None, :]]
    qg = q.reshape(B, rows, n_cb, Q_COLS, H, Dh)
    kgrid = k.reshape(B, rows, GRID_W, H, Dh)
    vgrid = v.reshape(B, rows, GRID_W, H, Dh)
    ri = row_idx[:, None, :, None]
    ci = col_idx[None, :, None, :]
    kg = kgrid[:, ri, ci]
    vg = vgrid[:, ri, ci]
    s = jnp.einsum('brcqhd,brckwhd->bhrcqkw', qg, kg,
                   preferred_element_type=jnp.float32) * (Dh ** -0.5)
    s = s + bias.astype(jnp.float32)[None]
    s = jnp.where(col_mask[:, :, None, :], s, -jnp.inf)
    p = jax.nn.softmax(s, axis=(-2, -1))
    o = jnp.einsum('bhrcqkw,brckwhd->brcqhd', p.astype(v.dtype), vg)
    return o.reshape(B, S, H * Dh)


def conformer_conv(a, gate, conv_w, conv_b, ln_g, ln_b):
    C = a.shape[-1]
    u = a * jax.nn.sigmoid(gate)
    dw = lax.conv_general_dilated(
        u, conv_w[:, None, :].astype(u.dtype), window_strides=(1,),
        padding=[(CONV_K // 2, CONV_K // 2)],
        dimension_numbers=('NWC', 'WIO', 'NWC'), feature_group_count=C)
    dw = dw + conv_b
    return jax.nn.silu(layer_norm(dw, ln_g, ln_b))


def expert_choice_ffn(h, w_router, w_gate, w_up, w_down):
    B, S, D = h.shape
    cap = EC_CAPACITY_FACTOR * S // N_EXPERTS
    logits = jnp.einsum('bsd,de->bse', h, w_router, preferred_element_type=jnp.float32)
    aff = jax.nn.softmax(logits, axis=-1)
    gates, idx = lax.top_k(jnp.swapaxes(aff, 1, 2), cap)
    bidx = jnp.arange(B)[:, None, None]
    xs = h[bidx, idx]
    hid = jax.nn.silu(jnp.einsum('becd,edf->becf', xs, w_gate)) * jnp.einsum('becd,edf->becf', xs, w_up)
    eo = jnp.einsum('becf,efd->becd', hid, w_down)
    return jnp.zeros_like(h).at[bidx, idx].add(gates[..., None].astype(h.dtype) * eo)


def setup_inputs(seed: int = 0) -> dict:
    key = jax.random.key(seed)
    ks = jax.random.split(key, 20)
    f32 = jnp.float32
    nrm = lambda k, shape, scale: jax.random.normal(k, shape, f32) * scale
    gain = lambda k, shape: 1.0 + 0.02 * jax.random.normal(k, shape, f32)
    return {
        "x": jax.random.normal(ks[0], (BATCH, SEQ, D_MODEL), f32),
        "norm1_g": gain(ks[1], (DEPTH, D_MODEL)),
        "w_in": nrm(ks[2], (DEPTH, D_MODEL, IN_WIDTH), D_MODEL ** -0.5),
        "rpb": nrm(ks[3], (DEPTH, ATTN_HEADS, 2 * WIN_ROWS - 1, 2 * WIN_COLS - 1), 0.02),
        "conv_w": nrm(ks[4], (DEPTH, CONV_K, CONV_WIDTH), CONV_K ** -0.5),
        "conv_b": nrm(ks[5], (DEPTH, CONV_WIDTH), 0.02),
        "conv_ln_g": gain(ks[6], (DEPTH, CONV_WIDTH)),
        "conv_ln_b": nrm(ks[7], (DEPTH, CONV_WIDTH), 0.02),
        "attn_out_g": gain(ks[8], (DEPTH, ATTN_WIDTH)),
        "conv_out_g": gain(ks[9], (DEPTH, CONV_WIDTH)),
        "w_out": nrm(ks[10], (DEPTH, D_MODEL, D_MODEL), (2 * DEPTH * D_MODEL) ** -0.5),
        "norm2_g": gain(ks[11], (DEPTH, D_MODEL)),
        "w_router": nrm(ks[12], (DEPTH, D_MODEL, N_EXPERTS), D_MODEL ** -0.5),
        "w_gate": nrm(ks[13], (DEPTH, N_EXPERTS, D_MODEL, EXPERT_FF), D_MODEL ** -0.5),
        "w_up": nrm(ks[14], (DEPTH, N_EXPERTS, D_MODEL, EXPERT_FF), D_MODEL ** -0.5),
        "w_down": nrm(ks[15], (DEPTH, N_EXPERTS, EXPERT_FF, D_MODEL), (2 * DEPTH * EXPERT_FF) ** -0.5),
        "final_g": gain(ks[16], (D_MODEL,)),
    }


def reference(x, norm1_g, w_in, rpb, conv_w, conv_b, conv_ln_g, conv_ln_b,
              attn_out_g, conv_out_g, w_out, norm2_g, w_router, w_gate, w_up,
              w_down, final_g):
    B, S, _ = x.shape
    splits = [ATTN_WIDTH, 2 * ATTN_WIDTH, 3 * ATTN_WIDTH, 3 * ATTN_WIDTH + CONV_WIDTH]
    for l in range(DEPTH):
        h = rms_norm(x, norm1_g[l])
        proj = jnp.einsum('bsd,de->bse', h, w_in[l])
        q, k, v, a, g = jnp.split(proj, splits, axis=-1)
        q = q.reshape(B, S, ATTN_HEADS, HEAD_DIM)
        k = k.reshape(B, S, ATTN_HEADS, HEAD_DIM)
        v = v.reshape(B, S, ATTN_HEADS, HEAD_DIM)
        attn = neighbourhood_attention(q, k, v, rpb[l])
        conv = conformer_conv(a, g, conv_w[l], conv_b[l], conv_ln_g[l], conv_ln_b[l])
        mixed = jnp.concatenate([rms_norm(attn, attn_out_g[l]), rms_norm(conv, conv_out_g[l])], axis=-1)
        x = x + jnp.einsum('bse,ed->bsd', mixed, w_out[l])
        h = rms_norm(x, norm2_g[l])
        x = x + expert_choice_ffn(h, w_router[l], w_gate[l], w_up[l], w_down[l])
    return rms_norm(x, final_g)
```

```python
import functools

import jax
import jax.numpy as jnp
from jax import lax
from jax.experimental import pallas as pl
from jax.experimental.pallas import tpu as pltpu

F32 = jnp.float32
BF16 = jnp.bfloat16

GRID_W = 64
HEADS = 8
HEAD_DIM = 64
ATTN_W = HEADS * HEAD_DIM
WIN_ROWS = 8
WIN_COLS = 16
CONV_K = 31
N_EXPERTS = 16
CAP_FACTOR = 2
EPS = 1e-6

LANES = 128
MIB = 1024 * 1024
VMEM_CAP = 56 * MIB

NEG_BIG = -1e30


def _params(vmem_bytes, sem=None):
    kw = dict(vmem_limit_bytes=int(min(max(vmem_bytes, 16 * MIB), VMEM_CAP)))
    if sem is not None:
        kw["dimension_semantics"] = sem
    return pltpu.CompilerParams(**kw)


def _rms(x, g):
    ms = jnp.mean(x * x, axis=-1, keepdims=True)
    return x * lax.rsqrt(ms + EPS) * g


def _in_proj_kernel(x_ref, g_ref, wqkv_ref, wag_ref, qkv_ref, u_ref, *, conv_w):
    h = _rms(x_ref[...], g_ref[...]).astype(BF16)
    qkv = jnp.dot(h, wqkv_ref[...], preferred_element_type=F32)
    qkv_ref[:, :ATTN_W] = (qkv[:, :ATTN_W] * (HEAD_DIM ** -0.5)).astype(BF16)
    qkv_ref[:, ATTN_W:] = qkv[:, ATTN_W:].astype(BF16)
    ag = jnp.dot(h, wag_ref[...], preferred_element_type=F32)
    u_ref[...] = ag[:, :conv_w] * jax.nn.sigmoid(ag[:, conv_w:])


def _in_proj(x2, g, w_in, *, tm=512):
    T, D = x2.shape
    conv_w = (w_in.shape[1] - 3 * ATTN_W) // 2
    wqkv = w_in[:, :3 * ATTN_W].astype(BF16)
    wag = w_in[:, 3 * ATTN_W:].astype(BF16)
    vmem = 2 * (tm * D * 4 + tm * 3 * ATTN_W * 2 + tm * conv_w * 4) \
        + 2 * D * w_in.shape[1] * 2 + 3 * tm * w_in.shape[1] * 4
    return pl.pallas_call(
        functools.partial(_in_proj_kernel, conv_w=conv_w),
        grid=(T // tm,),
        in_specs=[
            pl.BlockSpec((tm, D), lambda i: (i, 0)),
            pl.BlockSpec((1, D), lambda i: (0, 0)),
            pl.BlockSpec((D, 3 * ATTN_W), lambda i: (0, 0)),
            pl.BlockSpec((D, 2 * conv_w), lambda i: (0, 0)),
        ],
        out_specs=[
            pl.BlockSpec((tm, 3 * ATTN_W), lambda i: (i, 0)),
            pl.BlockSpec((tm, conv_w), lambda i: (i, 0)),
        ],
        out_shape=[
            jax.ShapeDtypeStruct((T, 3 * ATTN_W), BF16),
            jax.ShapeDtypeStruct((T, conv_w), F32),
        ],
        compiler_params=_params(vmem, ("arbitrary",)),
        name="in_proj",
    )(x2, g.reshape(1, D), wqkv, wag)


def _attn_bias_table(rpb, rows):
    kh = min(WIN_ROWS, rows)
    var = jnp.arange(kh)
    kr = jnp.arange(kh)
    row_off = kr[None, :] - var[:, None] + (WIN_ROWS - 1)
    qc = jnp.arange(GRID_W)
    kc = jnp.arange(GRID_W)
    win_start = jnp.clip(qc - WIN_COLS // 2, 0, GRID_W - WIN_COLS)
    mask = (kc[None, :] >= win_start[:, None]) & (kc[None, :] < win_start[:, None] + WIN_COLS)
    col_off = jnp.clip(kc[None, :] - qc[:, None] + (WIN_COLS - 1), 0, 2 * WIN_COLS - 2)
    b = rpb[:, row_off[:, None, :, None], col_off[None, :, None, :]]
    b = jnp.where(mask[None, None, :, None, :], b.astype(F32), NEG_BIG)
    b = b.reshape(HEADS, kh, GRID_W, kh * GRID_W)
    b = jnp.transpose(b, (1, 0, 2, 3))
    return b.reshape(kh, HEADS // 2, 2 * GRID_W, kh * GRID_W)


def _attn_kernel(qkv_ref, bias_ref, g_ref, o_ref, *, rows, kh):
    nk = kh * GRID_W
    lane = lax.broadcasted_iota(jnp.int32, (GRID_W, LANES), 1)
    first_head = lane < HEAD_DIM

    def row_body(r, carry):
        rs = jnp.clip(r - kh // 2, 0, rows - kh)
        var = r - rs
        q0 = pl.multiple_of(r * GRID_W, GRID_W)
        k0 = pl.multiple_of(rs * GRID_W, GRID_W)
        outs = []
        for p in range(HEADS // 2):
            c = p * LANES
            qp = qkv_ref[pl.ds(q0, GRID_W), c:c + LANES].astype(F32)
            qq = jnp.concatenate(
                [jnp.where(first_head, qp, 0.0), jnp.where(first_head, 0.0, qp)], axis=0
            ).astype(BF16)
            kp = qkv_ref[pl.ds(k0, nk), ATTN_W + c:ATTN_W + c + LANES]
            vp = qkv_ref[pl.ds(k0, nk), 2 * ATTN_W + c:2 * ATTN_W + c + LANES]
            s = lax.dot_general(qq, kp, (((1,), (1,)), ((), ())), preferred_element_type=F32)
            s = s + bias_ref[var, p]
            m = jnp.max(s, axis=-1, keepdims=True)
            e = jnp.exp(s - m)
            l = jnp.sum(e, axis=-1, keepdims=True)
            pv = jnp.dot(e.astype(BF16), vp, preferred_element_type=F32) / l
            outs.append(jnp.where(first_head, pv[:GRID_W], pv[GRID_W:]))
        o_row = jnp.concatenate(outs, axis=-1)
        o_ref[pl.ds(q0, GRID_W), :] = _rms(o_row, g_ref[...]).astype(BF16)
        return carry

    lax.fori_loop(0, rows, row_body, 0)


def _attn(qkv, bias, g, *, B, S):
    rows = S // GRID_W
    kh = min(WIN_ROWS, rows)
    vmem = 2 * (S * 3 * ATTN_W * 2 + S * ATTN_W * 2 + bias.size * 4) + 8 * MIB
    return pl.pallas_call(
        functools.partial(_attn_kernel, rows=rows, kh=kh),
        grid=(B,),
        in_specs=[
            pl.BlockSpec((S, 3 * ATTN_W), lambda b: (b, 0)),
            pl.BlockSpec(bias.shape, lambda b: (0, 0, 0, 0)),
            pl.BlockSpec((1, ATTN_W), lambda b: (0, 0)),
        ],
        out_specs=pl.BlockSpec((S, ATTN_W), lambda b: (b, 0)),
        out_shape=jax.ShapeDtypeStruct((B * S, ATTN_W), BF16),
        compiler_params=_params(vmem, ("arbitrary",)),
        name="attn",
    )(qkv, bias, g.reshape(1, ATTN_W))


CONV_PAD = 16
CONV_TT = 64


def _conv_kernel(u_ref, w_ref, cb_ref, lg_ref, lb_ref, og_ref, o_ref, pad_ref, *, S, C):
    for c in range(C // LANES):
        pad_ref[c, 0:CONV_PAD, :] = jnp.zeros((CONV_PAD, LANES), F32)
        pad_ref[c, CONV_PAD + S:, :] = jnp.zeros((CONV_PAD, LANES), F32)
        pad_ref[c, CONV_PAD:CONV_PAD + S, :] = u_ref[:, c * LANES:(c + 1) * LANES]
    shift = CONV_PAD - CONV_K // 2

    def tile(i, carry):
        t0 = pl.multiple_of(i * CONV_TT, CONV_TT)
        cols = []
        for c in range(C // LANES):
            sl = slice(c * LANES, (c + 1) * LANES)
            acc = jnp.zeros((CONV_TT, LANES), F32)
            for k in range(CONV_K):
                rows = pl.ds(t0 + shift + k, CONV_TT, stride=1)
                acc = acc + pad_ref[c, rows, :] * w_ref[k:k + 1, sl]
            cols.append(acc)
        dw = jnp.concatenate(cols, axis=-1) + cb_ref[...]
        mu = jnp.mean(dw, axis=-1, keepdims=True)
        d = dw - mu
        var = jnp.mean(d * d, axis=-1, keepdims=True)
        y = d * lax.rsqrt(var + EPS) * lg_ref[...] + lb_ref[...]
        z = y * jax.nn.sigmoid(y)
        o_ref[pl.ds(t0, CONV_TT), :] = _rms(z, og_ref[...]).astype(BF16)
        return carry

    lax.fori_loop(0, S // CONV_TT, tile, 0)


def _conv(u, w, cb, lg, lb, og, *, B, S):
    C = u.shape[1]
    vmem = 2 * (S * C * 4 + S * C * 2) + (S + 2 * CONV_PAD) * C * 4 + 4 * MIB
    row = lambda a: a.reshape(1, C)
    vec = pl.BlockSpec((1, C), lambda b: (0, 0))
    return pl.pallas_call(
        functools.partial(_conv_kernel, S=S, C=C),
        grid=(B,),
        in_specs=[
            pl.BlockSpec((S, C), lambda b: (b, 0)),
            pl.BlockSpec((CONV_K, C), lambda b: (0, 0)),
            vec, vec, vec, vec,
        ],
        out_specs=pl.BlockSpec((S, C), lambda b: (b, 0)),
        out_shape=jax.ShapeDtypeStruct((B * S, C), BF16),
        scratch_shapes=[pltpu.VMEM((C // LANES, S + 2 * CONV_PAD, LANES), F32)],
        compiler_params=_params(vmem, ("arbitrary",)),
        name="conv",
    )(u, w, row(cb), row(lg), row(lb), row(og))


def _out_proj_kernel(ma_ref, mc_ref, x_ref, wa_ref, wc_ref, g_ref, wr_ref,
                     x1_ref, h2_ref, aff_ref):
    acc = jnp.dot(ma_ref[...], wa_ref[...], preferred_element_type=F32)
    acc = acc + jnp.dot(mc_ref[...], wc_ref[...], preferred_element_type=F32)
    x1 = x_ref[...] + acc
    x1_ref[...] = x1
    h = _rms(x1, g_ref[...])
    h2_ref[...] = h.astype(BF16)
    logits = jnp.dot(h, wr_ref[...], preferred_element_type=F32,
                     precision=lax.Precision.HIGHEST)
    lane = lax.broadcasted_iota(jnp.int32, logits.shape, 1)
    logits = jnp.where(lane < N_EXPERTS, logits, NEG_BIG)
    e = jnp.exp(logits - jnp.max(logits, axis=-1, keepdims=True))
    aff = e / jnp.sum(e, axis=-1, keepdims=True)
    aff_ref[...] = aff[:, :N_EXPERTS]


def _out_proj(ma, mc, x2, w_out, g2, w_router, *, tm=512):
    T, D = x2.shape
    Wa = ma.shape[1]
    Wc = mc.shape[1]
    wa = w_out[:Wa].astype(BF16)
    wc = w_out[Wa:].astype(BF16)
    wr = jnp.zeros((D, LANES), F32).at[:, :N_EXPERTS].set(w_router)
    vmem = 2 * (tm * (Wa + Wc) * 2 + 2 * tm * D * 4 + tm * D * 2 + tm * LANES * 4) \
        + 2 * (D * D * 2 + D * LANES * 4) + 4 * tm * D * 4
    return pl.pallas_call(
        _out_proj_kernel,
        grid=(T // tm,),
        in_specs=[
            pl.BlockSpec((tm, Wa), lambda i: (i, 0)),
            pl.BlockSpec((tm, Wc), lambda i: (i, 0)),
            pl.BlockSpec((tm, D), lambda i: (i, 0)),
            pl.BlockSpec((Wa, D), lambda i: (0, 0)),
            pl.BlockSpec((Wc, D), lambda i: (0, 0)),
            pl.BlockSpec((1, D), lambda i: (0, 0)),
            pl.BlockSpec((D, LANES), lambda i: (0, 0)),
        ],
        out_specs=[
            pl.BlockSpec((tm, D), lambda i: (i, 0)),
            pl.BlockSpec((tm, D), lambda i: (i, 0)),
            pl.BlockSpec((tm, N_EXPERTS), lambda i: (i, 0)),
        ],
        out_shape=[
            jax.ShapeDtypeStruct((T, D), F32),
            jax.ShapeDtypeStruct((T, D), BF16),
            jax.ShapeDtypeStruct((T, N_EXPERTS), F32),
        ],
        compiler_params=_params(vmem, ("arbitrary",)),
        name="out_proj",
    )(ma, mc, x2, wa, wc, g2.reshape(1, D), wr)


ROUTE_BLK = 256


def _excl_cumsum_rows(mask_f32, tri):
    S = mask_f32.shape[0]
    carry = jnp.zeros((1, mask_f32.shape[1]), F32)
    out = []
    for i in range(S // ROUTE_BLK):
        blk = mask_f32[i * ROUTE_BLK:(i + 1) * ROUTE_BLK]
        out.append(jnp.dot(tri, blk.astype(BF16), preferred_element_type=F32) + carry)
        carry = carry + jnp.sum(blk, axis=0, keepdims=True)
    return jnp.concatenate(out, axis=0), carry


def _route_kernel(aff_ref, pos_ref, *, cap):
    S, N = aff_ref.shape
    bits = pltpu.bitcast(aff_ref[...], jnp.int32)

    def bisect(_, lh):
        lo, hi = lh
        mid = lo + ((hi - lo + 1) >> 1)
        cnt = jnp.sum((bits >= mid).astype(jnp.int32), axis=0, keepdims=True)
        ok = cnt >= cap
        return jnp.where(ok, mid, lo), jnp.where(ok, hi, mid - 1)

    lo0 = jnp.zeros((1, N), jnp.int32)
    hi0 = jnp.full((1, N), 0x7F800000, jnp.int32)
    thr, _ = lax.fori_loop(0, 31, bisect, (lo0, hi0))

    r = lax.broadcasted_iota(jnp.int32, (ROUTE_BLK, ROUTE_BLK), 0)
    c = lax.broadcasted_iota(jnp.int32, (ROUTE_BLK, ROUTE_BLK), 1)
    tri = jnp.where(r > c, 1.0, 0.0).astype(BF16)

    gt = jnp.where(bits > thr, 1.0, 0.0)
    eq = jnp.where(bits == thr, 1.0, 0.0)
    need = cap - jnp.sum(gt, axis=0, keepdims=True)
    eq_rank, _ = _excl_cumsum_rows(eq, tri)
    sel = gt + eq * jnp.where(eq_rank < need, 1.0, 0.0)
    pos, _ = _excl_cumsum_rows(sel, tri)
    pos_ref[...] = jnp.where(sel > 0.5, pos, -1.0).astype(jnp.int32)


def _route(aff_t, *, cap):
    S, N = aff_t.shape
    return pl.pallas_call(
        functools.partial(_route_kernel, cap=cap),
        out_shape=jax.ShapeDtypeStruct((S, N), jnp.int32),
        compiler_params=_params(32 * MIB),
        name="route",
    )(aff_t)


GATHER_EG = 8


def _gather_kernel(pos_ref, h_ref, xs_ref, p_ref, *, cap):
    S = h_ref.shape[0]
    slot = lax.broadcasted_iota(jnp.int32, (cap, S), 0)
    for e in range(GATHER_EG):
        hit = slot == pos_ref[0, e:e + 1, :]
        p_ref[e * cap:(e + 1) * cap, :] = jnp.where(hit, 1.0, 0.0).astype(BF16)
    xs = jnp.dot(p_ref[...], h_ref[...], preferred_element_type=F32)
    xs_ref[0] = xs.astype(BF16).reshape(GATHER_EG, cap, -1)


def _gather(pos_rows, h2, *, B, S, cap):
    D = h2.shape[1]
    E = pos_rows.shape[1]
    vmem = 2 * (GATHER_EG * S * 4 + S * D * 2 + GATHER_EG * cap * D * 2) \
        + GATHER_EG * cap * S * 2 + GATHER_EG * cap * (S + D) * 4
    return pl.pallas_call(
        functools.partial(_gather_kernel, cap=cap),
        grid=(B, E // GATHER_EG),
        in_specs=[
            pl.BlockSpec((1, GATHER_EG, S), lambda b, j: (b, j, 0)),
            pl.BlockSpec((S, D), lambda b, j: (b, 0)),
        ],
        out_specs=pl.BlockSpec((1, GATHER_EG, cap, D), lambda b, j: (b, j, 0, 0)),
        out_shape=jax.ShapeDtypeStruct((B, E, cap, D), BF16),
        scratch_shapes=[pltpu.VMEM((GATHER_EG * cap, S), BF16)],
        compiler_params=_params(vmem, ("arbitrary", "arbitrary")),
        name="gather",
    )(pos_rows, h2)


FFN_FC = 512
FFN_MC = 512


def _ffn_kernel(xs_ref, wg_ref, wu_ref, wd_ref, wr_ref, y_ref, acc_ref):
    e = pl.program_id(0)
    f = pl.program_id(1)
    nB, _, cap, D = xs_ref.shape
    M = nB * cap
    wg = wg_ref[0].astype(BF16)
    wu = wu_ref[0].astype(BF16)
    wd = wd_ref[0].astype(BF16)

    @pl.when(f == 0)
    def _():
        acc_ref[...] = jnp.zeros_like(acc_ref)

    for m in range(M // FFN_MC):
        per = FFN_MC // cap
        x = xs_ref[m * per:(m + 1) * per, 0].reshape(FFN_MC, D)
        g = jnp.dot(x, wg, preferred_element_type=F32)
        u = jnp.dot(x, wu, preferred_element_type=F32)
        hid = (g * jax.nn.sigmoid(g) * u).astype(BF16)
        acc_ref[m * FFN_MC:(m + 1) * FFN_MC, :] += jnp.dot(hid, wd, preferred_element_type=F32)

    @pl.when(f == pl.num_programs(1) - 1)
    def _():
        for m in range(M // FFN_MC):
            per = FFN_MC // cap
            x = xs_ref[m * per:(m + 1) * per, 0].reshape(FFN_MC, D)
            logits = jnp.dot(x, wr_ref[...], preferred_element_type=F32)
            lane = lax.broadcasted_iota(jnp.int32, logits.shape, 1)
            logits = jnp.where(lane < N_EXPERTS, logits, NEG_BIG)
            ex = jnp.exp(logits - jnp.max(logits, axis=-1, keepdims=True))
            gate = jnp.sum(jnp.where(lane == e, ex, 0.0), axis=-1, keepdims=True) \
                / jnp.sum(ex, axis=-1, keepdims=True)
            y = acc_ref[m * FFN_MC:(m + 1) * FFN_MC, :] * gate
            y_ref[m * per:(m + 1) * per, 0] = y.astype(BF16).reshape(per, cap, D)


def _ffn(xs, w_gate, w_up, w_down, w_router):
    B, E, cap, D = xs.shape
    FF = w_gate.shape[2]
    wr = jnp.zeros((D, LANES), BF16).at[:, :N_EXPERTS].set(w_router.astype(BF16))
    M = B * cap
    vmem = 4 * M * D * 2 + 6 * D * FFN_FC * 4 + 3 * D * FFN_FC * 2 + M * D * 4 \
        + 6 * FFN_MC * FFN_FC * 4 + 2 * FFN_MC * D * 4
    return pl.pallas_call(
        _ffn_kernel,
        grid=(E, FF // FFN_FC),
        in_specs=[
            pl.BlockSpec((B, 1, cap, D), lambda e, f: (0, e, 0, 0)),
            pl.BlockSpec((1, D, FFN_FC), lambda e, f: (e, 0, f)),
            pl.BlockSpec((1, D, FFN_FC), lambda e, f: (e, 0, f)),
            pl.BlockSpec((1, FFN_FC, D), lambda e, f: (e, f, 0)),
            pl.BlockSpec((D, LANES), lambda e, f: (0, 0)),
        ],
        out_specs=pl.BlockSpec((B, 1, cap, D), lambda e, f: (0, e, 0, 0)),
        out_shape=jax.ShapeDtypeStruct((B, E, cap, D), BF16),
        scratch_shapes=[pltpu.VMEM((M, D), F32)],
        compiler_params=_params(vmem, ("arbitrary", "arbitrary")),
        name="ffn",
    )(xs, w_gate, w_up, w_down, wr)


COMBINE_TT = 512


def _combine_kernel(pos_ref, y_ref, x_ref, fg_ref, o_ref, p_ref, *, cap, final):
    E = pos_ref.shape[2]
    slot = lax.broadcasted_iota(jnp.int32, (COMBINE_TT, cap), 1)
    pos = pos_ref[0]
    for e in range(E):
        hit = slot == pos[:, e:e + 1]
        p_ref[:, e * cap:(e + 1) * cap] = jnp.where(hit, 1.0, 0.0).astype(BF16)
    x2 = x_ref[...] + jnp.dot(p_ref[...], y_ref[0], preferred_element_type=F32)
    o_ref[...] = _rms(x2, fg_ref[...]) if final else x2


def _combine(pos_tok, y, x1, final_g, *, B, S, cap, final):
    E = pos_tok.shape[2]
    D = x1.shape[1]
    nt = S // COMBINE_TT
    vmem = 2 * (COMBINE_TT * LANES * 4 + E * cap * D * 2 + 2 * COMBINE_TT * D * 4) \
        + COMBINE_TT * E * cap * 2 + 2 * COMBINE_TT * D * 4 + 2 * COMBINE_TT * cap * 4
    return pl.pallas_call(
        functools.partial(_combine_kernel, cap=cap, final=final),
        grid=(B, nt),
        in_specs=[
            pl.BlockSpec((1, COMBINE_TT, E), lambda b, t: (b, t, 0)),
            pl.BlockSpec((1, E * cap, D), lambda b, t: (b, 0, 0)),
            pl.BlockSpec((COMBINE_TT, D), lambda b, t: (b * nt + t, 0)),
            pl.BlockSpec((1, D), lambda b, t: (0, 0)),
        ],
        out_specs=pl.BlockSpec((COMBINE_TT, D), lambda b, t: (b * nt + t, 0)),
        out_shape=jax.ShapeDtypeStruct((B * S, D), F32),
        scratch_shapes=[pltpu.VMEM((COMBINE_TT, E * cap), BF16)],
        compiler_params=_params(vmem, ("arbitrary", "arbitrary")),
        name="combine",
    )(pos_tok, y.reshape(B, E * cap, D), x1, final_g.reshape(1, D))


def kernel(x, norm1_g, w_in, rpb, conv_w, conv_b, conv_ln_g, conv_ln_b, attn_out_g,
           conv_out_g, w_out, norm2_g, w_router, w_gate, w_up, w_down, final_g):
    B, S, D = x.shape
    depth = w_in.shape[0]
    E = w_router.shape[2]
    cap = CAP_FACTOR * S // E
    rows = S // GRID_W
    x2 = x.reshape(B * S, D)
    for l in range(depth):
        qkv, u = _in_proj(x2, norm1_g[l], w_in[l])
        ma = _attn(qkv, _attn_bias_table(rpb[l], rows), attn_out_g[l], B=B, S=S)
        mc = _conv(u, conv_w[l], conv_b[l], conv_ln_g[l], conv_ln_b[l], conv_out_g[l], B=B, S=S)
        x1, h2, aff = _out_proj(ma, mc, x2, w_out[l], norm2_g[l], w_router[l])
        aff_t = jnp.transpose(aff.reshape(B, S, E), (1, 0, 2)).reshape(S, B * E)
        pos = _route(aff_t, cap=cap).reshape(S, B, E)
        pos_rows = jnp.transpose(pos, (1, 2, 0))
        pos_tok = jnp.transpose(pos, (1, 0, 2))
        xs = _gather(pos_rows, h2, B=B, S=S, cap=cap)
        y = _ffn(xs, w_gate[l], w_up[l], w_down[l], w_router[l])
        x2 = _combine(pos_tok, y, x1, final_g, B=B, S=S, cap=cap, final=(l == depth - 1))
    return x2.reshape(B, S, D)
```

```python
import functools

import jax
import jax.numpy as jnp
import numpy as np
from jax import lax
from jax.experimental import pallas as pl
from jax.experimental.pallas import tpu as pltpu

F32 = jnp.float32
BF16 = jnp.bfloat16

GRID_W = 64
HEADS = 8
HEAD_DIM = 64
ATTN_W = HEADS * HEAD_DIM
WIN_ROWS = 8
WIN_COLS = 16
CONV_K = 31
N_EXPERTS = 16
CAP_FACTOR = 2
EPS = 1e-6

LANES = 128
SUBLANES = 8
MIB = 1024 * 1024
VMEM_CAP = 56 * MIB

NEG_BIG = -1e30


def _params(vmem_bytes, sem=None):
    kw = dict(vmem_limit_bytes=int(min(max(vmem_bytes, 16 * MIB), VMEM_CAP)))
    if sem is not None:
        kw["dimension_semantics"] = sem
    return pltpu.CompilerParams(**kw)


def _rms(x, g):
    ms = jnp.mean(x * x, axis=-1, keepdims=True)
    return x * lax.rsqrt(ms + EPS) * g


def _layer_vec(l, n):
    return pl.BlockSpec((None, 1, n), lambda *_: (l, 0, 0))


def _in_proj_kernel(x_ref, g_ref, w_ref, qkv_ref, u_ref, *, conv_w):
    h = _rms(x_ref[...], g_ref[...]).astype(BF16)
    qkv = jnp.dot(h, w_ref[:, :3 * ATTN_W], preferred_element_type=F32)
    qkv_ref[:, :ATTN_W] = (qkv[:, :ATTN_W] * (HEAD_DIM ** -0.5)).astype(BF16)
    qkv_ref[:, ATTN_W:] = qkv[:, ATTN_W:].astype(BF16)
    ag = jnp.dot(h, w_ref[:, 3 * ATTN_W:], preferred_element_type=F32)
    u_ref[...] = ag[:, :conv_w] * jax.nn.sigmoid(ag[:, conv_w:])


def _in_proj(x2, g_all, w_all, l, *, tm=512):
    T, D = x2.shape
    N = w_all.shape[2]
    conv_w = (N - 3 * ATTN_W) // 2
    vmem = 2 * (tm * D * 4 + tm * 3 * ATTN_W * 2 + tm * conv_w * 4) + 2 * D * N * 2 + 3 * tm * N * 4
    return pl.pallas_call(
        functools.partial(_in_proj_kernel, conv_w=conv_w),
        grid=(T // tm,),
        in_specs=[
            pl.BlockSpec((tm, D), lambda i: (i, 0)),
            _layer_vec(l, D),
            pl.BlockSpec((None, D, N), lambda i: (l, 0, 0)),
        ],
        out_specs=[
            pl.BlockSpec((tm, 3 * ATTN_W), lambda i: (i, 0)),
            pl.BlockSpec((tm, conv_w), lambda i: (i, 0)),
        ],
        out_shape=[
            jax.ShapeDtypeStruct((T, 3 * ATTN_W), BF16),
            jax.ShapeDtypeStruct((T, conv_w), F32),
        ],
        compiler_params=_params(vmem, ("arbitrary",)),
        name="in_proj",
    )(x2, g_all, w_all)


def _attn_bias_tables(rpb_all, rows):
    kh = min(WIN_ROWS, rows)
    ncol = 2 * WIN_COLS - 1
    qc = np.arange(GRID_W)[:, None]
    kc = np.arange(GRID_W)[None, :]
    col_off = np.clip(kc - qc + (WIN_COLS - 1), 0, ncol - 1)
    sel = (col_off[None] == np.arange(ncol)[:, None, None]).astype(np.float32)
    win_start = np.clip(qc - WIN_COLS // 2, 0, GRID_W - WIN_COLS)
    mask = (kc >= win_start) & (kc < win_start + WIN_COLS)
    t = jnp.einsum("lhrc,cqk->lhrqk", rpb_all.astype(F32), jnp.asarray(sel),
                   precision=lax.Precision.HIGHEST)
    per_var = []
    for var in range(kh):
        r0 = WIN_ROWS - 1 - var
        b = jnp.transpose(t[:, :, r0:r0 + kh], (0, 1, 3, 2, 4))
        b = jnp.where(jnp.asarray(mask)[:, None, :], b, NEG_BIG)
        per_var.append(b.reshape(b.shape[0], HEADS // 2, 2 * GRID_W, kh * GRID_W))
    return jnp.stack(per_var, axis=1)


def _attn_kernel(qkv_ref, bias_ref, g_ref, o_ref, *, rows, kh):
    nk = kh * GRID_W
    lane = lax.broadcasted_iota(jnp.int32, (GRID_W, LANES), 1)
    first_head = lane < HEAD_DIM
    npair = HEADS // 2

    def row_body(r, carry):
        rs = jnp.clip(r - kh // 2, 0, rows - kh)
        var = r - rs
        q0 = pl.multiple_of(r * GRID_W, GRID_W)
        k0 = pl.multiple_of(rs * GRID_W, GRID_W)
        scores = []
        for p in range(npair):
            c = p * LANES
            qp = qkv_ref[pl.ds(q0, GRID_W), c:c + LANES].astype(F32)
            qq = jnp.concatenate(
                [jnp.where(first_head, qp, 0.0), jnp.where(first_head, 0.0, qp)], axis=0
            ).astype(BF16)
            kp = qkv_ref[pl.ds(k0, nk), ATTN_W + c:ATTN_W + c + LANES]
            scores.append(lax.dot_general(qq, kp, (((1,), (1,)), ((), ())),
                                          preferred_element_type=F32))
        probs, denoms = [], []
        for p in range(npair):
            s = scores[p] + bias_ref[var, p]
            e = jnp.exp(s - jnp.max(s, axis=-1, keepdims=True))
            denoms.append(jnp.sum(e, axis=-1, keepdims=True))
            probs.append(e.astype(BF16))
        outs = []
        for p in range(npair):
            c = 2 * ATTN_W + p * LANES
            vp = qkv_ref[pl.ds(k0, nk), c:c + LANES]
            pv = jnp.dot(probs[p], vp, preferred_element_type=F32) / denoms[p]
            outs.append(jnp.where(first_head, pv[:GRID_W], pv[GRID_W:]))
        o_row = jnp.concatenate(outs, axis=-1)
        o_ref[pl.ds(q0, GRID_W), :] = _rms(o_row, g_ref[...]).astype(BF16)
        return carry

    lax.fori_loop(0, rows, row_body, 0, unroll=2)


def _attn(qkv, bias_all, g_all, l, *, B, S):
    rows = S // GRID_W
    kh = min(WIN_ROWS, rows)
    tab = bias_all.shape[1:]
    vmem = 2 * (S * 3 * ATTN_W * 2 + S * ATTN_W * 2 + int(np.prod(tab)) * 4) + 8 * MIB
    return pl.pallas_call(
        functools.partial(_attn_kernel, rows=rows, kh=kh),
        grid=(B,),
        in_specs=[
            pl.BlockSpec((S, 3 * ATTN_W), lambda b: (b, 0)),
            pl.BlockSpec((None,) + tab, lambda b: (l, 0, 0, 0, 0)),
            _layer_vec(l, ATTN_W),
        ],
        out_specs=pl.BlockSpec((S, ATTN_W), lambda b: (b, 0)),
        out_shape=jax.ShapeDtypeStruct((B * S, ATTN_W), BF16),
        compiler_params=_params(vmem, ("arbitrary",)),
        name="attn",
    )(qkv, bias_all, g_all)


CONV_PAD = 16
CONV_TT = 64


def _conv_kernel(u_ref, w_ref, cb_ref, lg_ref, lb_ref, og_ref, o_ref, pad_ref, dw_ref, *, S, C):
    shift = CONV_PAD - CONV_K // 2
    nsub = CONV_TT // SUBLANES
    for c in range(C // LANES):
        sl = slice(c * LANES, (c + 1) * LANES)
        pad_ref[c, 0:CONV_PAD, :] = jnp.zeros((CONV_PAD, LANES), F32)
        pad_ref[c, CONV_PAD + S:, :] = jnp.zeros((CONV_PAD, LANES), F32)
        pad_ref[c, CONV_PAD:CONV_PAD + S, :] = u_ref[:, sl]
        wb = [jnp.broadcast_to(w_ref[k:k + 1, sl], (SUBLANES, LANES)) for k in range(CONV_K)]

        def tile(i, carry, c=c, sl=sl, wb=wb):
            t0 = pl.multiple_of(i * CONV_TT, CONV_TT)
            accs = [jnp.zeros((SUBLANES, LANES), F32) for _ in range(nsub)]
            for off in range(SUBLANES * (nsub - 1) + CONV_K):
                win = pad_ref[c, pl.ds(t0 + shift + off, SUBLANES, stride=1), :]
                for j in range(nsub):
                    k = off - SUBLANES * j
                    if 0 <= k < CONV_K:
                        accs[j] = accs[j] + win * wb[k]
            dw_ref[pl.ds(t0, CONV_TT), sl] = jnp.concatenate(accs, axis=0)
            return carry

        lax.fori_loop(0, S // CONV_TT, tile, 0)

    def norm_tile(i, carry):
        t0 = pl.multiple_of(i * CONV_TT, CONV_TT)
        dw = dw_ref[pl.ds(t0, CONV_TT), :] + cb_ref[...]
        mu = jnp.mean(dw, axis=-1, keepdims=True)
        d = dw - mu
        var = jnp.mean(d * d, axis=-1, keepdims=True)
        y = d * lax.rsqrt(var + EPS) * lg_ref[...] + lb_ref[...]
        z = y * jax.nn.sigmoid(y)
        o_ref[pl.ds(t0, CONV_TT), :] = _rms(z, og_ref[...]).astype(BF16)
        return carry

    lax.fori_loop(0, S // CONV_TT, norm_tile, 0, unroll=4)


def _conv(u, w_all, cb_all, lg_all, lb_all, og_all, l, *, B, S):
    C = u.shape[1]
    vmem = 2 * (S * C * 4 + S * C * 2) + (S + 2 * CONV_PAD) * C * 4 + S * C * 4 + 4 * MIB
    vec = _layer_vec(l, C)
    return pl.pallas_call(
        functools.partial(_conv_kernel, S=S, C=C),
        grid=(B,),
        in_specs=[
            pl.BlockSpec((S, C), lambda b: (b, 0)),
            pl.BlockSpec((None, CONV_K, C), lambda b: (l, 0, 0)),
            vec, vec, vec, vec,
        ],
        out_specs=pl.BlockSpec((S, C), lambda b: (b, 0)),
        out_shape=jax.ShapeDtypeStruct((B * S, C), BF16),
        scratch_shapes=[pltpu.VMEM((C // LANES, S + 2 * CONV_PAD, LANES), F32),
                        pltpu.VMEM((S, C), F32)],
        compiler_params=_params(vmem, ("arbitrary",)),
        name="conv",
    )(u, w_all, cb_all, lg_all, lb_all, og_all)


def _router_weights(w_router_all):
    L, D, E = w_router_all.shape
    hi = w_router_all.astype(BF16)
    lo = (w_router_all - hi.astype(F32)).astype(BF16)
    pad = jnp.zeros((L, D, LANES - 2 * E), BF16)
    return jnp.concatenate([hi, lo, pad], axis=-1)


def _router_logits(h_bf16, wr):
    a = jnp.dot(h_bf16, wr, preferred_element_type=F32)
    return a + pltpu.roll(a, LANES - N_EXPERTS, axis=1)


def _out_proj_kernel(ma_ref, mc_ref, x_ref, w_ref, g_ref, wr_ref, x1_ref, h2_ref, aff_ref):
    wa = ma_ref.shape[1]
    acc = jnp.dot(ma_ref[...], w_ref[:wa], preferred_element_type=F32)
    acc = acc + jnp.dot(mc_ref[...], w_ref[wa:], preferred_element_type=F32)
    x1 = x_ref[...] + acc
    x1_ref[...] = x1
    h = _rms(x1, g_ref[...]).astype(BF16)
    h2_ref[...] = h
    logits = _router_logits(h, wr_ref[...])
    lane = lax.broadcasted_iota(jnp.int32, logits.shape, 1)
    logits = jnp.where(lane < N_EXPERTS, logits, NEG_BIG)
    e = jnp.exp(logits - jnp.max(logits, axis=-1, keepdims=True))
    aff = e / jnp.sum(e, axis=-1, keepdims=True)
    aff_ref[...] = aff[:, :N_EXPERTS]


def _out_proj(ma, mc, x2, w_all, g_all, wr_all, l, *, tm=512):
    T, D = x2.shape
    Wa = ma.shape[1]
    Wc = mc.shape[1]
    vmem = 2 * (tm * (Wa + Wc) * 2 + 2 * tm * D * 4 + tm * D * 2 + tm * LANES * 4) \
        + 2 * (D * D * 2 + D * LANES * 2) + 4 * tm * D * 4
    return pl.pallas_call(
        _out_proj_kernel,
        grid=(T // tm,),
        in_specs=[
            pl.BlockSpec((tm, Wa), lambda i: (i, 0)),
            pl.BlockSpec((tm, Wc), lambda i: (i, 0)),
            pl.BlockSpec((tm, D), lambda i: (i, 0)),
            pl.BlockSpec((None, Wa + Wc, D), lambda i: (l, 0, 0)),
            _layer_vec(l, D),
            pl.BlockSpec((None, D, LANES), lambda i: (l, 0, 0)),
        ],
        out_specs=[
            pl.BlockSpec((tm, D), lambda i: (i, 0)),
            pl.BlockSpec((tm, D), lambda i: (i, 0)),
            pl.BlockSpec((tm, N_EXPERTS), lambda i: (i, 0)),
        ],
        out_shape=[
            jax.ShapeDtypeStruct((T, D), F32),
            jax.ShapeDtypeStruct((T, D), BF16),
            jax.ShapeDtypeStruct((T, N_EXPERTS), F32),
        ],
        compiler_params=_params(vmem, ("arbitrary",)),
        name="out_proj",
    )(ma, mc, x2, w_all, g_all, wr_all)


ROUTE_BLK = 256


def _excl_cumsum_rows(mask_f32, tri):
    S = mask_f32.shape[0]
    carry = jnp.zeros((1, mask_f32.shape[1]), F32)
    out = []
    for i in range(S // ROUTE_BLK):
        blk = mask_f32[i * ROUTE_BLK:(i + 1) * ROUTE_BLK]
        out.append(jnp.dot(tri, blk.astype(BF16), preferred_element_type=F32) + carry)
        carry = carry + jnp.sum(blk, axis=0, keepdims=True)
    return jnp.concatenate(out, axis=0)


def _route_kernel(aff_ref, pos_tok_ref, pos_rows_ref, *, cap):
    B, S, E = aff_ref.shape
    N = B * E
    aff = jnp.concatenate([aff_ref[b] for b in range(B)], axis=-1)
    bits = pltpu.bitcast(aff, jnp.int32)

    def bisect(_, lh):
        lo, hi = lh
        mid = lo + ((hi - lo + 1) >> 1)
        cnt = jnp.sum((bits >= mid).astype(jnp.int32), axis=0, keepdims=True)
        ok = cnt >= cap
        return jnp.where(ok, mid, lo), jnp.where(ok, hi, mid - 1)

    lo0 = jnp.zeros((1, N), jnp.int32)
    hi0 = jnp.full((1, N), 0x7F800000, jnp.int32)
    thr, _ = lax.fori_loop(0, 31, bisect, (lo0, hi0))

    r = lax.broadcasted_iota(jnp.int32, (ROUTE_BLK, ROUTE_BLK), 0)
    c = lax.broadcasted_iota(jnp.int32, (ROUTE_BLK, ROUTE_BLK), 1)
    tri = jnp.where(r > c, 1.0, 0.0).astype(BF16)

    gt = jnp.where(bits > thr, 1.0, 0.0)
    eq = jnp.where(bits == thr, 1.0, 0.0)
    need = cap - jnp.sum(gt, axis=0, keepdims=True)
    sel = gt + eq * jnp.where(_excl_cumsum_rows(eq, tri) < need, 1.0, 0.0)
    pos = jnp.where(sel > 0.5, _excl_cumsum_rows(sel, tri), -1.0)
    for b in range(B):
        pos_tok_ref[b] = pos[:, b * E:(b + 1) * E].astype(jnp.int32)
    pos_rows_ref[...] = pos.T.astype(jnp.int32)


def _route(aff, *, cap):
    B, S, E = aff.shape
    return pl.pallas_call(
        functools.partial(_route_kernel, cap=cap),
        out_shape=[jax.ShapeDtypeStruct((B, S, E), jnp.int32),
                   jax.ShapeDtypeStruct((B * E, S), jnp.int32)],
        compiler_params=_params(32 * MIB),
        name="route",
    )(aff)


GATHER_EG = 8


def _gather_kernel(pos_ref, h_ref, xs_ref, p_ref, *, cap):
    S = h_ref.shape[0]
    slot = lax.broadcasted_iota(jnp.int32, (cap, S), 0)
    for e in range(GATHER_EG):
        hit = slot == pos_ref[e:e + 1, :]
        p_ref[e * cap:(e + 1) * cap, :] = jnp.where(hit, 1.0, 0.0).astype(BF16)
    xs = jnp.dot(p_ref[...], h_ref[...], preferred_element_type=F32)
    xs_ref[0] = xs.astype(BF16).reshape(GATHER_EG, cap, -1)


def _gather(pos_rows, h2, *, B, S, E, cap):
    D = h2.shape[1]
    ng = E // GATHER_EG
    vmem = 2 * (GATHER_EG * S * 4 + S * D * 2 + GATHER_EG * cap * D * 2) \
        + GATHER_EG * cap * S * 2 + GATHER_EG * cap * (S + D) * 4
    return pl.pallas_call(
        functools.partial(_gather_kernel, cap=cap),
        grid=(B, ng),
        in_specs=[
            pl.BlockSpec((GATHER_EG, S), lambda b, j: (b * ng + j, 0)),
            pl.BlockSpec((S, D), lambda b, j: (b, 0)),
        ],
        out_specs=pl.BlockSpec((1, GATHER_EG, cap, D), lambda b, j: (b, j, 0, 0)),
        out_shape=jax.ShapeDtypeStruct((B, E, cap, D), BF16),
        scratch_shapes=[pltpu.VMEM((GATHER_EG * cap, S), BF16)],
        compiler_params=_params(vmem, ("arbitrary", "arbitrary")),
        name="gather",
    )(pos_rows, h2)


FFN_FC = 512
FFN_MC = 512


def _ffn_kernel(xs_ref, wg_ref, wu_ref, wd_ref, wr_ref, y_ref, acc_ref):
    e = pl.program_id(0)
    f = pl.program_id(1)
    nB, _, cap, D = xs_ref.shape
    per = FFN_MC // cap
    nm = nB // per
    wg = wg_ref[...].astype(BF16)
    wu = wu_ref[...].astype(BF16)
    wd = wd_ref[...].astype(BF16)

    def rows(m):
        return xs_ref[m * per:(m + 1) * per, 0].reshape(FFN_MC, D)

    @pl.when(f == 0)
    def _():
        acc_ref[...] = jnp.zeros_like(acc_ref)

    def gate_up(m):
        x = rows(m)
        return (jnp.dot(x, wg, preferred_element_type=F32),
                jnp.dot(x, wu, preferred_element_type=F32))

    gu = gate_up(0)
    for m in range(nm):
        nxt = gate_up(m + 1) if m + 1 < nm else None
        g, u = gu
        hid = (g * jax.nn.sigmoid(g) * u).astype(BF16)
        acc_ref[m * FFN_MC:(m + 1) * FFN_MC, :] += jnp.dot(hid, wd, preferred_element_type=F32)
        gu = nxt

    @pl.when(f == pl.num_programs(1) - 1)
    def _():
        for m in range(nm):
            logits = _router_logits(rows(m), wr_ref[...])
            lane = lax.broadcasted_iota(jnp.int32, logits.shape, 1)
            logits = jnp.where(lane < N_EXPERTS, logits, NEG_BIG)
            ex = jnp.exp(logits - jnp.max(logits, axis=-1, keepdims=True))
            gate = jnp.sum(jnp.where(lane == e, ex, 0.0), axis=-1, keepdims=True) \
                / jnp.sum(ex, axis=-1, keepdims=True)
            y = acc_ref[m * FFN_MC:(m + 1) * FFN_MC, :] * gate
            y_ref[m * per:(m + 1) * per, 0] = y.astype(BF16).reshape(per, cap, D)


def _ffn(xs, wg_all, wu_all, wd_all, wr_all, l):
    B, E, cap, D = xs.shape
    FF = wg_all.shape[3]
    M = B * cap
    vmem = 4 * M * D * 2 + 6 * D * FFN_FC * 4 + 3 * D * FFN_FC * 2 + M * D * 4 \
        + 8 * FFN_MC * FFN_FC * 4 + 2 * FFN_MC * D * 4
    return pl.pallas_call(
        _ffn_kernel,
        grid=(E, FF // FFN_FC),
        in_specs=[
            pl.BlockSpec((B, 1, cap, D), lambda e, f: (0, e, 0, 0)),
            pl.BlockSpec((None, None, D, FFN_FC), lambda e, f: (l, e, 0, f)),
            pl.BlockSpec((None, None, D, FFN_FC), lambda e, f: (l, e, 0, f)),
            pl.BlockSpec((None, None, FFN_FC, D), lambda e, f: (l, e, f, 0)),
            pl.BlockSpec((None, D, LANES), lambda e, f: (l, 0, 0)),
        ],
        out_specs=pl.BlockSpec((B, 1, cap, D), lambda e, f: (0, e, 0, 0)),
        out_shape=jax.ShapeDtypeStruct((B, E, cap, D), BF16),
        scratch_shapes=[pltpu.VMEM((M, D), F32)],
        compiler_params=_params(vmem, ("arbitrary", "arbitrary")),
        name="ffn",
    )(xs, wg_all, wu_all, wd_all, wr_all)


COMBINE_TT = 512


def _combine_kernel(pos_ref, y_ref, x_ref, fg_ref, o_ref, p_ref, *, cap, final):
    E = pos_ref.shape[2]
    slot = lax.broadcasted_iota(jnp.int32, (COMBINE_TT, cap), 1)
    pos = pos_ref[0]
    for e in range(E):
        hit = slot == pos[:, e:e + 1]
        p_ref[:, e * cap:(e + 1) * cap] = jnp.where(hit, 1.0, 0.0).astype(BF16)
    x2 = x_ref[...] + jnp.dot(p_ref[...], y_ref[0], preferred_element_type=F32)
    o_ref[...] = _rms(x2, fg_ref[...]) if final else x2


def _combine(pos_tok, y, x1, final_g, *, B, S, cap, final):
    E = pos_tok.shape[2]
    D = x1.shape[1]
    nt = S // COMBINE_TT
    vmem = 2 * (COMBINE_TT * LANES * 4 + E * cap * D * 2 + 2 * COMBINE_TT * D * 4) \
        + COMBINE_TT * E * cap * 2 + 2 * COMBINE_TT * D * 4 + 2 * COMBINE_TT * cap * 4
    return pl.pallas_call(
        functools.partial(_combine_kernel, cap=cap, final=final),
        grid=(B, nt),
        in_specs=[
            pl.BlockSpec((1, COMBINE_TT, E), lambda b, t: (b, t, 0)),
            pl.BlockSpec((1, E * cap, D), lambda b, t: (b, 0, 0)),
            pl.BlockSpec((COMBINE_TT, D), lambda b, t: (b * nt + t, 0)),
            pl.BlockSpec((1, D), lambda b, t: (0, 0)),
        ],
        out_specs=pl.BlockSpec((COMBINE_TT, D), lambda b, t: (b * nt + t, 0)),
        out_shape=jax.ShapeDtypeStruct((B * S, D), F32),
        scratch_shapes=[pltpu.VMEM((COMBINE_TT, E * cap), BF16)],
        compiler_params=_params(vmem, ("arbitrary", "arbitrary")),
        name="combine",
    )(pos_tok, y.reshape(B, E * cap, D), x1, final_g.reshape(1, D))


def kernel(x, norm1_g, w_in, rpb, conv_w, conv_b, conv_ln_g, conv_ln_b, attn_out_g,
           conv_out_g, w_out, norm2_g, w_router, w_gate, w_up, w_down, final_g):
    B, S, D = x.shape
    depth = w_in.shape[0]
    E = w_router.shape[2]
    cap = CAP_FACTOR * S // E
    stack = lambda a: a.reshape(depth, 1, a.shape[-1])
    w_in_bf = w_in.astype(BF16)
    w_out_bf = w_out.astype(BF16)
    wr = _router_weights(w_router)
    bias = _attn_bias_tables(rpb, S // GRID_W)
    x2 = x.reshape(B * S, D)
    for l in range(depth):
        qkv, u = _in_proj(x2, stack(norm1_g), w_in_bf, l)
        ma = _attn(qkv, bias, stack(attn_out_g), l, B=B, S=S)
        mc = _conv(u, conv_w, stack(conv_b), stack(conv_ln_g), stack(conv_ln_b),
                   stack(conv_out_g), l, B=B, S=S)
        x1, h2, aff = _out_proj(ma, mc, x2, w_out_bf, stack(norm2_g), wr, l)
        pos_tok, pos_rows = _route(aff.reshape(B, S, E), cap=cap)
        xs = _gather(pos_rows, h2, B=B, S=S, E=E, cap=cap)
        y = _ffn(xs, w_gate, w_up, w_down, wr, l)
        x2 = _combine(pos_tok, y, x1, final_g, B=B, S=S, cap=cap, final=(l == depth - 1))
    return x2.reshape(B, S, D)
```

```python
import functools

import jax
import jax.numpy as jnp
import numpy as np
from jax import lax
from jax.experimental import pallas as pl
from jax.experimental.pallas import tpu as pltpu

F32 = jnp.float32
BF16 = jnp.bfloat16

GRID_W = 64
HEADS = 8
HEAD_DIM = 64
ATTN_W = HEADS * HEAD_DIM
WIN_ROWS = 8
WIN_COLS = 16
CONV_K = 31
N_EXPERTS = 16
CAP_FACTOR = 2
EPS = 1e-6

LANES = 128
SUBLANES = 8
MIB = 1024 * 1024
VMEM_CAP = 56 * MIB

NEG_BIG = -1e30


def _params(vmem_bytes, sem=None):
    kw = dict(vmem_limit_bytes=int(min(max(vmem_bytes, 16 * MIB), VMEM_CAP)))
    if sem is not None:
        kw["dimension_semantics"] = sem
    return pltpu.CompilerParams(**kw)


def _rms(x, g):
    ms = jnp.mean(x * x, axis=-1, keepdims=True)
    return x * lax.rsqrt(ms + EPS) * g


def _layer_vec(l, n):
    return pl.BlockSpec((None, 1, n), lambda *_: (l, 0, 0))


PROJ_SUB = 512


def _in_proj_kernel(x_ref, g_ref, w_ref, qkv_ref, u_ref, *, conv_w):
    subs = [slice(i, i + PROJ_SUB) for i in range(0, x_ref.shape[0], PROJ_SUB)]
    hs = [_rms(x_ref[r, :], g_ref[...]).astype(BF16) for r in subs]
    qkvs = [jnp.dot(h, w_ref[:, :3 * ATTN_W], preferred_element_type=F32) for h in hs]
    ags = [jnp.dot(h, w_ref[:, 3 * ATTN_W:], preferred_element_type=F32) for h in hs]
    for r, qkv, ag in zip(subs, qkvs, ags):
        qkv_ref[r, :ATTN_W] = (qkv[:, :ATTN_W] * (HEAD_DIM ** -0.5)).astype(BF16)
        qkv_ref[r, ATTN_W:] = qkv[:, ATTN_W:].astype(BF16)
        u_ref[r, :] = ag[:, :conv_w] * jax.nn.sigmoid(ag[:, conv_w:])


def _in_proj(x2, g_all, w_all, l, *, tm=2 * PROJ_SUB):
    T, D = x2.shape
    N = w_all.shape[2]
    conv_w = (N - 3 * ATTN_W) // 2
    vmem = 2 * (tm * D * 4 + tm * 3 * ATTN_W * 2 + tm * conv_w * 4) + 2 * D * N * 2 \
        + tm * N * 4 + tm * D * 2 + 4 * MIB
    return pl.pallas_call(
        functools.partial(_in_proj_kernel, conv_w=conv_w),
        grid=(T // tm,),
        in_specs=[
            pl.BlockSpec((tm, D), lambda i: (i, 0)),
            _layer_vec(l, D),
            pl.BlockSpec((None, D, N), lambda i: (l, 0, 0)),
        ],
        out_specs=[
            pl.BlockSpec((tm, 3 * ATTN_W), lambda i: (i, 0)),
            pl.BlockSpec((tm, conv_w), lambda i: (i, 0)),
        ],
        out_shape=[
            jax.ShapeDtypeStruct((T, 3 * ATTN_W), BF16),
            jax.ShapeDtypeStruct((T, conv_w), F32),
        ],
        compiler_params=_params(vmem, ("arbitrary",)),
        name="in_proj",
    )(x2, g_all, w_all)


def _attn_bias_tables(rpb_all, rows):
    kh = min(WIN_ROWS, rows)
    ncol = 2 * WIN_COLS - 1
    nrow = 2 * WIN_ROWS - 1
    qc = np.arange(GRID_W)[:, None]
    kc = np.arange(GRID_W)[None, :]
    col_off = np.clip(kc - qc + (WIN_COLS - 1), 0, ncol - 1)
    col_sel = (col_off[None] == np.arange(ncol)[:, None, None]).astype(np.float32)
    var = np.arange(kh)[:, None]
    kr = np.arange(kh)[None, :]
    row_off = kr - var + (WIN_ROWS - 1)
    row_sel = (row_off[:, :, None] == np.arange(nrow)[None, None, :]).astype(np.float32)
    win_start = np.clip(qc - WIN_COLS // 2, 0, GRID_W - WIN_COLS)
    mask = (kc >= win_start) & (kc < win_start + WIN_COLS)
    L = rpb_all.shape[0]
    rows_picked = jnp.einsum("lhrc,vkr->lvhkc", rpb_all.astype(F32), jnp.asarray(row_sel),
                             precision=lax.Precision.HIGHEST)
    t = jnp.einsum("lvhkc,cqj->lvhqkj", rows_picked, jnp.asarray(col_sel),
                   precision=lax.Precision.HIGHEST)
    t = jnp.where(jnp.asarray(mask)[:, None, :], t, NEG_BIG)
    return t.reshape(L, kh, HEADS // 2, 2 * GRID_W, kh * GRID_W)


def _attn_kernel(qkv_ref, bias_ref, g_ref, o_ref, s0, s1, p0, p1, l0, l1, *, rows, kh):
    nk = kh * GRID_W
    npair = HEADS // 2
    lane = lax.broadcasted_iota(jnp.int32, (GRID_W, LANES), 1)
    first_head = lane < HEAD_DIM
    s_buf, p_buf, l_buf = (s0, s1), (p0, p1), (l0, l1)

    def key_row(r):
        lo, hi = 0, rows - kh
        return min(max(r - kh // 2, lo), hi) if isinstance(r, int) else jnp.clip(r - kh // 2, lo, hi)

    def row_start(r):
        return r * GRID_W if isinstance(r, int) else pl.multiple_of(r * GRID_W, GRID_W)

    def scores(r, slot):
        q0, k0 = row_start(r), row_start(key_row(r))
        for p in range(npair):
            c = p * LANES
            qp = qkv_ref[pl.ds(q0, GRID_W), c:c + LANES].astype(F32)
            qq = jnp.concatenate(
                [jnp.where(first_head, qp, 0.0), jnp.where(first_head, 0.0, qp)], axis=0
            ).astype(BF16)
            kp = qkv_ref[pl.ds(k0, nk), ATTN_W + c:ATTN_W + c + LANES]
            s_buf[slot][p] = lax.dot_general(qq, kp, (((1,), (1,)), ((), ())),
                                             preferred_element_type=F32)

    def softmax(r, slot):
        var = r - key_row(r)
        for p in range(npair):
            s = s_buf[slot][p] + bias_ref[var, p]
            e = jnp.exp(s - jnp.max(s, axis=-1, keepdims=True))
            l_buf[slot][p] = jnp.broadcast_to(jnp.sum(e, axis=-1, keepdims=True), (2 * GRID_W, LANES))
            p_buf[slot][p] = e.astype(BF16)

    def values(r, slot):
        q0, k0 = row_start(r), row_start(key_row(r))
        outs = []
        for p in range(npair):
            c = 2 * ATTN_W + p * LANES
            vp = qkv_ref[pl.ds(k0, nk), c:c + LANES]
            pv = jnp.dot(p_buf[slot][p], vp, preferred_element_type=F32) / l_buf[slot][p]
            outs.append(jnp.where(first_head, pv[:GRID_W], pv[GRID_W:]))
        o_row = jnp.concatenate(outs, axis=-1)
        o_ref[pl.ds(q0, GRID_W), :] = _rms(o_row, g_ref[...]).astype(BF16)

    scores(0, 0)
    scores(1, 1)
    softmax(0, 0)

    def body(j, carry):
        i = 2 * j
        scores(i, 0)
        softmax(i - 1, 1)
        values(i - 2, 0)
        scores(i + 1, 1)
        softmax(i, 0)
        values(i - 1, 1)
        return carry

    lax.fori_loop(1, rows // 2, body, 0)
    softmax(rows - 1, 1)
    values(rows - 2, 0)
    values(rows - 1, 1)


def _attn(qkv, bias_all, g_all, l, *, B, S):
    rows = S // GRID_W
    kh = min(WIN_ROWS, rows)
    assert rows % 2 == 0 and rows >= 4, "the row pipeline alternates two buffer sets"
    tab = bias_all.shape[1:]
    nk = kh * GRID_W
    npair = HEADS // 2
    ring = 2 * npair * 2 * GRID_W * (nk * 4 + nk * 2 + LANES * 4)
    vmem = 2 * (S * 3 * ATTN_W * 2 + S * ATTN_W * 2 + int(np.prod(tab)) * 4) + ring + 8 * MIB
    buf = lambda n, dt: pltpu.VMEM((npair, 2 * GRID_W, n), dt)
    return pl.pallas_call(
        functools.partial(_attn_kernel, rows=rows, kh=kh),
        grid=(B,),
        in_specs=[
            pl.BlockSpec((S, 3 * ATTN_W), lambda b: (b, 0)),
            pl.BlockSpec((None,) + tab, lambda b: (l, 0, 0, 0, 0)),
            _layer_vec(l, ATTN_W),
        ],
        out_specs=pl.BlockSpec((S, ATTN_W), lambda b: (b, 0)),
        out_shape=jax.ShapeDtypeStruct((B * S, ATTN_W), BF16),
        scratch_shapes=[buf(nk, F32), buf(nk, F32), buf(nk, BF16), buf(nk, BF16),
                        buf(LANES, F32), buf(LANES, F32)],
        compiler_params=_params(vmem, ("arbitrary",)),
        name="attn",
    )(qkv, bias_all, g_all)


CONV_PAD = 16
CONV_TT = 64


def _conv_kernel(u_ref, w_ref, cb_ref, lg_ref, lb_ref, og_ref, o_ref, pad_ref, dw_ref, *, S, C):
    shift = CONV_PAD - CONV_K // 2
    nsub = CONV_TT // SUBLANES
    for c in range(C // LANES):
        sl = slice(c * LANES, (c + 1) * LANES)
        pad_ref[c, 0:CONV_PAD, :] = jnp.zeros((CONV_PAD, LANES), F32)
        pad_ref[c, CONV_PAD + S:, :] = jnp.zeros((CONV_PAD, LANES), F32)
        pad_ref[c, CONV_PAD:CONV_PAD + S, :] = u_ref[:, sl]
        wb = [jnp.broadcast_to(w_ref[k:k + 1, sl], (SUBLANES, LANES)) for k in range(CONV_K)]

        def tile(i, carry, c=c, sl=sl, wb=wb):
            t0 = pl.multiple_of(i * CONV_TT, CONV_TT)
            accs = [jnp.zeros((SUBLANES, LANES), F32) for _ in range(nsub)]
            for off in range(SUBLANES * (nsub - 1) + CONV_K):
                win = pad_ref[c, pl.ds(t0 + shift + off, SUBLANES, stride=1), :]
                for j in range(nsub):
                    k = off - SUBLANES * j
                    if 0 <= k < CONV_K:
                        accs[j] = accs[j] + win * wb[k]
            dw_ref[pl.ds(t0, CONV_TT), sl] = jnp.concatenate(accs, axis=0)
            return carry

        lax.fori_loop(0, S // CONV_TT, tile, 0)

    def norm_tile(i, carry):
        t0 = pl.multiple_of(i * CONV_TT, CONV_TT)
        dw = dw_ref[pl.ds(t0, CONV_TT), :] + cb_ref[...]
        mu = jnp.mean(dw, axis=-1, keepdims=True)
        d = dw - mu
        var = jnp.mean(d * d, axis=-1, keepdims=True)
        y = d * lax.rsqrt(var + EPS) * lg_ref[...] + lb_ref[...]
        z = y * jax.nn.sigmoid(y)
        o_ref[pl.ds(t0, CONV_TT), :] = _rms(z, og_ref[...]).astype(BF16)
        return carry

    lax.fori_loop(0, S // CONV_TT, norm_tile, 0, unroll=4)


def _conv(u, w_all, cb_all, lg_all, lb_all, og_all, l, *, B, S):
    C = u.shape[1]
    vmem = 2 * (S * C * 4 + S * C * 2) + (S + 2 * CONV_PAD) * C * 4 + S * C * 4 + 4 * MIB
    vec = _layer_vec(l, C)
    return pl.pallas_call(
        functools.partial(_conv_kernel, S=S, C=C),
        grid=(B,),
        in_specs=[
            pl.BlockSpec((S, C), lambda b: (b, 0)),
            pl.BlockSpec((None, CONV_K, C), lambda b: (l, 0, 0)),
            vec, vec, vec, vec,
        ],
        out_specs=pl.BlockSpec((S, C), lambda b: (b, 0)),
        out_shape=jax.ShapeDtypeStruct((B * S, C), BF16),
        scratch_shapes=[pltpu.VMEM((C // LANES, S + 2 * CONV_PAD, LANES), F32),
                        pltpu.VMEM((S, C), F32)],
        compiler_params=_params(vmem, ("arbitrary",)),
        name="conv",
    )(u, w_all, cb_all, lg_all, lb_all, og_all)


def _router_weights(w_router_all):
    wt = jnp.swapaxes(w_router_all, 1, 2)
    hi = wt.astype(BF16)
    lo = (wt - hi.astype(F32)).astype(BF16)
    return jnp.concatenate([hi, lo], axis=1)


def _router_affinity_t(h_bf16, wr_t):
    lt = lax.dot_general(wr_t, h_bf16, (((1,), (1,)), ((), ())), preferred_element_type=F32)
    logits = lt[:N_EXPERTS] + lt[N_EXPERTS:]
    e = jnp.exp(logits - jnp.max(logits, axis=0, keepdims=True))
    return e / jnp.sum(e, axis=0, keepdims=True)


def _out_proj_kernel(ma_ref, mc_ref, x_ref, w_ref, g_ref, wr_ref, x1_ref, h2_ref, aff_ref):
    wa = ma_ref.shape[1]
    subs = [slice(i, i + PROJ_SUB) for i in range(0, x_ref.shape[0], PROJ_SUB)]
    accs = [jnp.dot(ma_ref[r, :], w_ref[:wa], preferred_element_type=F32)
            + jnp.dot(mc_ref[r, :], w_ref[wa:], preferred_element_type=F32) for r in subs]
    for r, acc in zip(subs, accs):
        x1 = x_ref[r, :] + acc
        x1_ref[r, :] = x1
        h = _rms(x1, g_ref[...]).astype(BF16)
        h2_ref[r, :] = h
        aff_ref[:, r] = _router_affinity_t(h, wr_ref[...])


def _out_proj(ma, mc, x2, w_all, g_all, wr_all, l, *, B, S, tm=2 * PROJ_SUB):
    T, D = x2.shape
    Wa = ma.shape[1]
    Wc = mc.shape[1]
    nt = S // tm
    vmem = 2 * (tm * (Wa + Wc) * 2 + 2 * tm * D * 4 + tm * D * 2 + N_EXPERTS * tm * 4) \
        + 2 * (D * D * 2 + 2 * N_EXPERTS * D * 2) + 2 * tm * D * 4 + tm * D * 2 + 4 * MIB
    return pl.pallas_call(
        _out_proj_kernel,
        grid=(T // tm,),
        in_specs=[
            pl.BlockSpec((tm, Wa), lambda i: (i, 0)),
            pl.BlockSpec((tm, Wc), lambda i: (i, 0)),
            pl.BlockSpec((tm, D), lambda i: (i, 0)),
            pl.BlockSpec((None, Wa + Wc, D), lambda i: (l, 0, 0)),
            _layer_vec(l, D),
            pl.BlockSpec((None, 2 * N_EXPERTS, D), lambda i: (l, 0, 0)),
        ],
        out_specs=[
            pl.BlockSpec((tm, D), lambda i: (i, 0)),
            pl.BlockSpec((tm, D), lambda i: (i, 0)),
            pl.BlockSpec((None, N_EXPERTS, tm), lambda i: (i // nt, 0, i % nt)),
        ],
        out_shape=[
            jax.ShapeDtypeStruct((T, D), F32),
            jax.ShapeDtypeStruct((T, D), BF16),
            jax.ShapeDtypeStruct((B, N_EXPERTS, S), F32),
        ],
        compiler_params=_params(vmem, ("arbitrary",)),
        name="out_proj",
    )(ma, mc, x2, w_all, g_all, wr_all)


ROUTE_BLK = 256


def _excl_cumsum_rows(mask_f32, tri):
    S = mask_f32.shape[0]
    carry = jnp.zeros((1, mask_f32.shape[1]), F32)
    out = []
    for i in range(S // ROUTE_BLK):
        blk = mask_f32[i * ROUTE_BLK:(i + 1) * ROUTE_BLK]
        out.append(jnp.dot(tri, blk.astype(BF16), preferred_element_type=F32) + carry)
        carry = carry + jnp.sum(blk, axis=0, keepdims=True)
    return jnp.concatenate(out, axis=0)


def _route_kernel(aff_ref, pos_tok_ref, pos_rows_ref, *, cap):
    B, E, S = aff_ref.shape
    N = B * E
    aff = aff_ref[...].reshape(N, S).T
    bits = pltpu.bitcast(aff, jnp.int32)

    def bisect(_, lh):
        lo, hi = lh
        mid = lo + ((hi - lo + 1) >> 1)
        cnt = jnp.sum((bits >= mid).astype(jnp.int32), axis=0, keepdims=True)
        ok = cnt >= cap
        return jnp.where(ok, mid, lo), jnp.where(ok, hi, mid - 1)

    lo0 = jnp.zeros((1, N), jnp.int32)
    hi0 = jnp.full((1, N), 0x7F800000, jnp.int32)
    thr, _ = lax.fori_loop(0, 31, bisect, (lo0, hi0))

    r = lax.broadcasted_iota(jnp.int32, (ROUTE_BLK, ROUTE_BLK), 0)
    c = lax.broadcasted_iota(jnp.int32, (ROUTE_BLK, ROUTE_BLK), 1)
    tri = jnp.where(r > c, 1.0, 0.0).astype(BF16)

    gt = jnp.where(bits > thr, 1.0, 0.0)
    eq = jnp.where(bits == thr, 1.0, 0.0)
    need = cap - jnp.sum(gt, axis=0, keepdims=True)
    sel = gt + eq * jnp.where(_excl_cumsum_rows(eq, tri) < need, 1.0, 0.0)
    pos = jnp.where(sel > 0.5, _excl_cumsum_rows(sel, tri), -1.0)
    for b in range(B):
        pos_tok_ref[b] = pos[:, b * E:(b + 1) * E].astype(jnp.int32)
    pos_rows_ref[...] = pos.T.astype(jnp.int32)


def _route(aff_t, *, cap):
    B, E, S = aff_t.shape
    return pl.pallas_call(
        functools.partial(_route_kernel, cap=cap),
        out_shape=[jax.ShapeDtypeStruct((B, S, E), jnp.int32),
                   jax.ShapeDtypeStruct((B * E, S), jnp.int32)],
        compiler_params=_params(32 * MIB),
        name="route",
    )(aff_t)


GATHER_EG = 8


def _gather_kernel(pos_ref, h_ref, xs_ref, p_ref, *, cap):
    S = h_ref.shape[0]
    slot = lax.broadcasted_iota(jnp.int32, (cap, S), 0)
    for e in range(GATHER_EG):
        hit = slot == pos_ref[e:e + 1, :]
        p_ref[e * cap:(e + 1) * cap, :] = jnp.where(hit, 1.0, 0.0).astype(BF16)
    xs = jnp.dot(p_ref[...], h_ref[...], preferred_element_type=F32)
    xs_ref[0] = xs.astype(BF16).reshape(GATHER_EG, cap, -1)


def _gather(pos_rows, h2, *, B, S, E, cap):
    D = h2.shape[1]
    ng = E // GATHER_EG
    vmem = 2 * (GATHER_EG * S * 4 + S * D * 2 + GATHER_EG * cap * D * 2) \
        + GATHER_EG * cap * S * 2 + GATHER_EG * cap * (S + D) * 4
    return pl.pallas_call(
        functools.partial(_gather_kernel, cap=cap),
        grid=(B, ng),
        in_specs=[
            pl.BlockSpec((GATHER_EG, S), lambda b, j: (b * ng + j, 0)),
            pl.BlockSpec((S, D), lambda b, j: (b, 0)),
        ],
        out_specs=pl.BlockSpec((1, GATHER_EG, cap, D), lambda b, j: (b, j, 0, 0)),
        out_shape=jax.ShapeDtypeStruct((B, E, cap, D), BF16),
        scratch_shapes=[pltpu.VMEM((GATHER_EG * cap, S), BF16)],
        compiler_params=_params(vmem, ("arbitrary", "arbitrary")),
        name="gather",
    )(pos_rows, h2)


FFN_FC = 512
FFN_MC = 512


def _ffn_kernel(xs_ref, wg_ref, wu_ref, wd_ref, wr_ref, y_ref, acc_ref):
    e = pl.program_id(0)
    f = pl.program_id(1)
    nB, _, cap, D = xs_ref.shape
    per = FFN_MC // cap
    nm = nB // per
    wg = wg_ref[...].astype(BF16)
    wu = wu_ref[...].astype(BF16)
    wd = wd_ref[...].astype(BF16)

    @pl.when(f == 0)
    def _():
        acc_ref[...] = jnp.zeros_like(acc_ref)

    def rows(m):
        return xs_ref[m * per:(m + 1) * per, 0].reshape(FFN_MC, D)

    def gate_up(m):
        x = rows(m)
        return (jnp.dot(x, wg, preferred_element_type=F32),
                jnp.dot(x, wu, preferred_element_type=F32))

    gu = gate_up(0)
    for m in range(nm):
        nxt = gate_up(m + 1) if m + 1 < nm else None
        g, u = gu
        hid = (g * jax.nn.sigmoid(g) * u).astype(BF16)
        acc_ref[m * FFN_MC:(m + 1) * FFN_MC, :] += jnp.dot(hid, wd, preferred_element_type=F32)
        gu = nxt

    @pl.when(f == pl.num_programs(1) - 1)
    def _():
        for m in range(nm):
            aff_t = _router_affinity_t(rows(m), wr_ref[...])
            expert = lax.broadcasted_iota(jnp.int32, aff_t.shape, 0)
            gate_row = jnp.sum(jnp.where(expert == e, aff_t, 0.0), axis=0, keepdims=True)
            gate = jnp.broadcast_to(gate_row, (LANES, FFN_MC)).T
            y = acc_ref[m * FFN_MC:(m + 1) * FFN_MC, :] * jnp.tile(gate, (1, D // LANES))
            y_ref[m * per:(m + 1) * per, 0] = y.astype(BF16).reshape(per, cap, D)


def _ffn(xs, wg_all, wu_all, wd_all, wr_all, l):
    B, E, cap, D = xs.shape
    FF = wg_all.shape[3]
    M = B * cap
    vmem = 4 * M * D * 2 + 6 * D * FFN_FC * 4 + 3 * D * FFN_FC * 2 + M * D * 4 \
        + 8 * FFN_MC * FFN_FC * 4 + 2 * FFN_MC * D * 4
    return pl.pallas_call(
        _ffn_kernel,
        grid=(E, FF // FFN_FC),
        in_specs=[
            pl.BlockSpec((B, 1, cap, D), lambda e, f: (0, e, 0, 0)),
            pl.BlockSpec((None, None, D, FFN_FC), lambda e, f: (l, e, 0, f)),
            pl.BlockSpec((None, None, D, FFN_FC), lambda e, f: (l, e, 0, f)),
            pl.BlockSpec((None, None, FFN_FC, D), lambda e, f: (l, e, f, 0)),
            pl.BlockSpec((None, 2 * N_EXPERTS, D), lambda e, f: (l, 0, 0)),
        ],
        out_specs=pl.BlockSpec((B, 1, cap, D), lambda e, f: (0, e, 0, 0)),
        out_shape=jax.ShapeDtypeStruct((B, E, cap, D), BF16),
        scratch_shapes=[pltpu.VMEM((M, D), F32)],
        compiler_params=_params(vmem, ("arbitrary", "arbitrary")),
        name="ffn",
    )(xs, wg_all, wu_all, wd_all, wr_all)


COMBINE_TT = 512


def _combine_kernel(pos_ref, y_ref, x_ref, fg_ref, o_ref, p_ref, *, cap, final):
    E = pos_ref.shape[2]
    slot = lax.broadcasted_iota(jnp.int32, (COMBINE_TT, cap), 1)
    pos = pos_ref[0]
    for e in range(E):
        hit = slot == pos[:, e:e + 1]
        p_ref[:, e * cap:(e + 1) * cap] = jnp.where(hit, 1.0, 0.0).astype(BF16)
    x2 = x_ref[...] + jnp.dot(p_ref[...], y_ref[0], preferred_element_type=F32)
    o_ref[...] = _rms(x2, fg_ref[...]) if final else x2


def _combine(pos_tok, y, x1, final_g, *, B, S, cap, final):
    E = pos_tok.shape[2]
    D = x1.shape[1]
    nt = S // COMBINE_TT
    vmem = 2 * (COMBINE_TT * LANES * 4 + E * cap * D * 2 + 2 * COMBINE_TT * D * 4) \
        + COMBINE_TT * E * cap * 2 + 2 * COMBINE_TT * D * 4 + 2 * COMBINE_TT * cap * 4
    return pl.pallas_call(
        functools.partial(_combine_kernel, cap=cap, final=final),
        grid=(B, nt),
        in_specs=[
            pl.BlockSpec((1, COMBINE_TT, E), lambda b, t: (b, t, 0)),
            pl.BlockSpec((1, E * cap, D), lambda b, t: (b, 0, 0)),
            pl.BlockSpec((COMBINE_TT, D), lambda b, t: (b * nt + t, 0)),
            pl.BlockSpec((1, D), lambda b, t: (0, 0)),
        ],
        out_specs=pl.BlockSpec((COMBINE_TT, D), lambda b, t: (b * nt + t, 0)),
        out_shape=jax.ShapeDtypeStruct((B * S, D), F32),
        scratch_shapes=[pltpu.VMEM((COMBINE_TT, E * cap), BF16)],
        compiler_params=_params(vmem, ("arbitrary", "arbitrary")),
        name="combine",
    )(pos_tok, y.reshape(B, E * cap, D), x1, final_g.reshape(1, D))


def kernel(x, norm1_g, w_in, rpb, conv_w, conv_b, conv_ln_g, conv_ln_b, attn_out_g,
           conv_out_g, w_out, norm2_g, w_router, w_gate, w_up, w_down, final_g):
    B, S, D = x.shape
    depth = w_in.shape[0]
    E = w_router.shape[2]
    cap = CAP_FACTOR * S // E
    stack = lambda a: a.reshape(depth, 1, a.shape[-1])
    w_in_bf = w_in.astype(BF16)
    w_out_bf = w_out.astype(BF16)
    wr = _router_weights(w_router)
    bias = _attn_bias_tables(rpb, S // GRID_W)
    x2 = x.reshape(B * S, D)
    for l in range(depth):
        qkv, u = _in_proj(x2, stack(norm1_g), w_in_bf, l)
        ma = _attn(qkv, bias, stack(attn_out_g), l, B=B, S=S)
        mc = _conv(u, conv_w, stack(conv_b), stack(conv_ln_g), stack(conv_ln_b),
                   stack(conv_out_g), l, B=B, S=S)
        x1, h2, aff = _out_proj(ma, mc, x2, w_out_bf, stack(norm2_g), wr, l, B=B, S=S)
        pos_tok, pos_rows = _route(aff, cap=cap)
        xs = _gather(pos_rows, h2, B=B, S=S, E=E, cap=cap)
        y = _ffn(xs, w_gate, w_up, w_down, wr, l)
        x2 = _combine(pos_tok, y, x1, final_g, B=B, S=S, cap=cap, final=(l == depth - 1))
    return x2.reshape(B, S, D)
```

```python
import functools

import jax
import jax.numpy as jnp
import numpy as np
from jax import lax
from jax.experimental import pallas as pl
from jax.experimental.pallas import tpu as pltpu

F32 = jnp.float32
BF16 = jnp.bfloat16

GRID_W = 64
HEADS = 8
HEAD_DIM = 64
ATTN_W = HEADS * HEAD_DIM
WIN_ROWS = 8
WIN_COLS = 16
CONV_K = 31
N_EXPERTS = 16
CAP_FACTOR = 2
EPS = 1e-6
LOG2E = 1.4426950408889634

LANES = 128
SUBLANES = 8
MIB = 1024 * 1024
VMEM_CAP = 56 * MIB

NEG_BIG = -1e30


def _params(vmem_bytes, sem=None):
    kw = dict(vmem_limit_bytes=int(min(max(vmem_bytes, 16 * MIB), VMEM_CAP)))
    if sem is not None:
        kw["dimension_semantics"] = sem
    return pltpu.CompilerParams(**kw)


def _rms(x, g):
    ms = jnp.mean(x * x, axis=-1, keepdims=True)
    return x * lax.rsqrt(ms + EPS) * g


def _layer_vec(l, n):
    return pl.BlockSpec((None, 1, n), lambda *_: (l, 0, 0))


PROJ_SUB = 512


def _in_proj_kernel(x_ref, g_ref, w_ref, qkv_ref, u_ref, *, conv_w):
    subs = [slice(i, i + PROJ_SUB) for i in range(0, x_ref.shape[0], PROJ_SUB)]
    hs = [_rms(x_ref[r, :], g_ref[...]).astype(BF16) for r in subs]
    qkvs = [jnp.dot(h, w_ref[:, :3 * ATTN_W], preferred_element_type=F32) for h in hs]
    ags = [jnp.dot(h, w_ref[:, 3 * ATTN_W:], preferred_element_type=F32) for h in hs]
    for r, qkv, ag in zip(subs, qkvs, ags):
        qkv_ref[r, :ATTN_W] = (qkv[:, :ATTN_W] * (HEAD_DIM ** -0.5 * LOG2E)).astype(BF16)
        qkv_ref[r, ATTN_W:] = qkv[:, ATTN_W:].astype(BF16)
        u_ref[r, :] = ag[:, :conv_w] * jax.nn.sigmoid(ag[:, conv_w:])


def _in_proj(x2, g_all, w_all, l, *, tm=2 * PROJ_SUB):
    T, D = x2.shape
    N = w_all.shape[2]
    conv_w = (N - 3 * ATTN_W) // 2
    vmem = 2 * (tm * D * 4 + tm * 3 * ATTN_W * 2 + tm * conv_w * 4) + 2 * D * N * 2 \
        + tm * N * 4 + tm * D * 2 + 4 * MIB
    return pl.pallas_call(
        functools.partial(_in_proj_kernel, conv_w=conv_w),
        grid=(T // tm,),
        in_specs=[
            pl.BlockSpec((tm, D), lambda i: (i, 0)),
            _layer_vec(l, D),
            pl.BlockSpec((None, D, N), lambda i: (l, 0, 0)),
        ],
        out_specs=[
            pl.BlockSpec((tm, 3 * ATTN_W), lambda i: (i, 0)),
            pl.BlockSpec((tm, conv_w), lambda i: (i, 0)),
        ],
        out_shape=[
            jax.ShapeDtypeStruct((T, 3 * ATTN_W), BF16),
            jax.ShapeDtypeStruct((T, conv_w), F32),
        ],
        compiler_params=_params(vmem, ("arbitrary",)),
        name="in_proj",
    )(x2, g_all, w_all)


def _attn_bias_tables(rpb_all, rows):
    kh = min(WIN_ROWS, rows)
    ncol = 2 * WIN_COLS - 1
    nrow = 2 * WIN_ROWS - 1
    qc = np.arange(GRID_W)[:, None]
    kc = np.arange(GRID_W)[None, :]
    col_off = np.clip(kc - qc + (WIN_COLS - 1), 0, ncol - 1)
    col_sel = (col_off[None] == np.arange(ncol)[:, None, None]).astype(np.float32)
    var = np.arange(kh)[:, None]
    kr = np.arange(kh)[None, :]
    row_off = kr - var + (WIN_ROWS - 1)
    row_sel = (row_off[:, :, None] == np.arange(nrow)[None, None, :]).astype(np.float32)
    win_start = np.clip(qc - WIN_COLS // 2, 0, GRID_W - WIN_COLS)
    mask = (kc >= win_start) & (kc < win_start + WIN_COLS)
    L = rpb_all.shape[0]
    rows_picked = jnp.einsum("lhrc,vkr->lvhkc", rpb_all.astype(F32), jnp.asarray(row_sel),
                             precision=lax.Precision.HIGHEST)
    t = jnp.einsum("lvhkc,cqj->lvhqkj", rows_picked, jnp.asarray(col_sel),
                   precision=lax.Precision.HIGHEST)
    t = jnp.where(jnp.asarray(mask)[:, None, :], t * LOG2E, NEG_BIG)
    return t.reshape(L, kh, HEADS // 2, 2 * GRID_W, kh * GRID_W)


def _attn_kernel(qkv_ref, bias_ref, g_ref, o_ref, s0, s1, p0, p1, l0, l1, *, rows, kh):
    nk = kh * GRID_W
    npair = HEADS // 2
    lane = lax.broadcasted_iota(jnp.int32, (GRID_W, LANES), 1)
    first_head = lane < HEAD_DIM
    s_buf, p_buf, l_buf = (s0, s1), (p0, p1), (l0, l1)

    def key_row(r):
        lo, hi = 0, rows - kh
        return min(max(r - kh // 2, lo), hi) if isinstance(r, int) else jnp.clip(r - kh // 2, lo, hi)

    def row_start(r):
        return r * GRID_W if isinstance(r, int) else pl.multiple_of(r * GRID_W, GRID_W)

    def scores(r, slot):
        q0, k0 = row_start(r), row_start(key_row(r))
        for p in range(npair):
            c = p * LANES
            qp = qkv_ref[pl.ds(q0, GRID_W), c:c + LANES].astype(F32)
            qq = jnp.concatenate(
                [jnp.where(first_head, qp, 0.0), jnp.where(first_head, 0.0, qp)], axis=0
            ).astype(BF16)
            kp = qkv_ref[pl.ds(k0, nk), ATTN_W + c:ATTN_W + c + LANES]
            s_buf[slot][p] = lax.dot_general(qq, kp, (((1,), (1,)), ((), ())),
                                             preferred_element_type=F32)

    def softmax(r, slot):
        var = r - key_row(r)
        for p in range(npair):
            s = s_buf[slot][p] + bias_ref[var, p]
            e = jnp.exp2(s - jnp.max(s, axis=-1, keepdims=True))
            l_buf[slot][p] = jnp.broadcast_to(jnp.sum(e, axis=-1, keepdims=True), (2 * GRID_W, LANES))
            p_buf[slot][p] = e.astype(BF16)

    def values(r, slot):
        q0, k0 = row_start(r), row_start(key_row(r))
        outs = []
        for p in range(npair):
            c = 2 * ATTN_W + p * LANES
            vp = qkv_ref[pl.ds(k0, nk), c:c + LANES]
            pv = jnp.dot(p_buf[slot][p], vp, preferred_element_type=F32) / l_buf[slot][p]
            outs.append(jnp.where(first_head, pv[:GRID_W], pv[GRID_W:]))
        o_row = jnp.concatenate(outs, axis=-1)
        o_ref[pl.ds(q0, GRID_W), :] = _rms(o_row, g_ref[...]).astype(BF16)

    scores(0, 0)
    scores(1, 1)
    softmax(0, 0)

    def body(j, carry):
        i = 2 * j
        scores(i, 0)
        softmax(i - 1, 1)
        values(i - 2, 0)
        scores(i + 1, 1)
        softmax(i, 0)
        values(i - 1, 1)
        return carry

    lax.fori_loop(1, rows // 2, body, 0)
    softmax(rows - 1, 1)
    values(rows - 2, 0)
    values(rows - 1, 1)


def _attn(qkv, bias_all, g_all, l, *, B, S):
    rows = S // GRID_W
    kh = min(WIN_ROWS, rows)
    assert rows % 2 == 0 and rows >= 4, "the row pipeline alternates two buffer sets"
    tab = bias_all.shape[1:]
    nk = kh * GRID_W
    npair = HEADS // 2
    ring = 2 * npair * 2 * GRID_W * (nk * 4 + nk * 2 + LANES * 4)
    vmem = 2 * (S * 3 * ATTN_W * 2 + S * ATTN_W * 2 + int(np.prod(tab)) * 4) + ring + 8 * MIB
    buf = lambda n, dt: pltpu.VMEM((npair, 2 * GRID_W, n), dt)
    return pl.pallas_call(
        functools.partial(_attn_kernel, rows=rows, kh=kh),
        grid=(B,),
        in_specs=[
            pl.BlockSpec((S, 3 * ATTN_W), lambda b: (b, 0)),
            pl.BlockSpec((None,) + tab, lambda b: (l, 0, 0, 0, 0)),
            _layer_vec(l, ATTN_W),
        ],
        out_specs=pl.BlockSpec((S, ATTN_W), lambda b: (b, 0)),
        out_shape=jax.ShapeDtypeStruct((B * S, ATTN_W), BF16),
        scratch_shapes=[buf(nk, F32), buf(nk, F32), buf(nk, BF16), buf(nk, BF16),
                        buf(LANES, F32), buf(LANES, F32)],
        compiler_params=_params(vmem, ("arbitrary",)),
        name="attn",
    )(qkv, bias_all, g_all)


CONV_PAD = 16
CONV_TT = 64


def _conv_kernel(u_ref, w_ref, cb_ref, lg_ref, lb_ref, og_ref, o_ref, pad_ref, dw_ref, *, S, C):
    shift = CONV_PAD - CONV_K // 2
    nsub = CONV_TT // SUBLANES
    for c in range(C // LANES):
        sl = slice(c * LANES, (c + 1) * LANES)
        pad_ref[c, 0:CONV_PAD, :] = jnp.zeros((CONV_PAD, LANES), F32)
        pad_ref[c, CONV_PAD + S:, :] = jnp.zeros((CONV_PAD, LANES), F32)
        pad_ref[c, CONV_PAD:CONV_PAD + S, :] = u_ref[:, sl]
        wb = [jnp.broadcast_to(w_ref[k:k + 1, sl], (SUBLANES, LANES)) for k in range(CONV_K)]

        def tile(i, carry, c=c, sl=sl, wb=wb):
            t0 = pl.multiple_of(i * CONV_TT, CONV_TT)
            accs = [jnp.zeros((SUBLANES, LANES), F32) for _ in range(nsub)]
            for off in range(SUBLANES * (nsub - 1) + CONV_K):
                win = pad_ref[c, pl.ds(t0 + shift + off, SUBLANES, stride=1), :]
                for j in range(nsub):
                    k = off - SUBLANES * j
                    if 0 <= k < CONV_K:
                        accs[j] = accs[j] + win * wb[k]
            dw_ref[pl.ds(t0, CONV_TT), sl] = jnp.concatenate(accs, axis=0)
            return carry

        lax.fori_loop(0, S // CONV_TT, tile, 0)

    def norm_tile(i, carry):
        t0 = pl.multiple_of(i * CONV_TT, CONV_TT)
        dw = dw_ref[pl.ds(t0, CONV_TT), :] + cb_ref[...]
        mu = jnp.mean(dw, axis=-1, keepdims=True)
        d = dw - mu
        var = jnp.mean(d * d, axis=-1, keepdims=True)
        y = d * lax.rsqrt(var + EPS) * lg_ref[...] + lb_ref[...]
        z = y * jax.nn.sigmoid(y)
        o_ref[pl.ds(t0, CONV_TT), :] = _rms(z, og_ref[...]).astype(BF16)
        return carry

    lax.fori_loop(0, S // CONV_TT, norm_tile, 0, unroll=4)


def _conv(u, w_all, cb_all, lg_all, lb_all, og_all, l, *, B, S):
    C = u.shape[1]
    vmem = 2 * (S * C * 4 + S * C * 2) + (S + 2 * CONV_PAD) * C * 4 + S * C * 4 + 4 * MIB
    vec = _layer_vec(l, C)
    return pl.pallas_call(
        functools.partial(_conv_kernel, S=S, C=C),
        grid=(B,),
        in_specs=[
            pl.BlockSpec((S, C), lambda b: (b, 0)),
            pl.BlockSpec((None, CONV_K, C), lambda b: (l, 0, 0)),
            vec, vec, vec, vec,
        ],
        out_specs=pl.BlockSpec((S, C), lambda b: (b, 0)),
        out_shape=jax.ShapeDtypeStruct((B * S, C), BF16),
        scratch_shapes=[pltpu.VMEM((C // LANES, S + 2 * CONV_PAD, LANES), F32),
                        pltpu.VMEM((S, C), F32)],
        compiler_params=_params(vmem, ("arbitrary",)),
        name="conv",
    )(u, w_all, cb_all, lg_all, lb_all, og_all)


def _router_weights(w_router_all):
    wt = jnp.swapaxes(w_router_all, 1, 2)
    hi = wt.astype(BF16)
    lo = (wt - hi.astype(F32)).astype(BF16)
    return jnp.concatenate([hi, lo], axis=1)


def _router_affinity_t(h_bf16, wr_t):
    lt = lax.dot_general(wr_t, h_bf16, (((1,), (1,)), ((), ())), preferred_element_type=F32)
    logits = lt[:N_EXPERTS] + lt[N_EXPERTS:]
    e = jnp.exp(logits - jnp.max(logits, axis=0, keepdims=True))
    return e / jnp.sum(e, axis=0, keepdims=True)


def _out_proj_kernel(ma_ref, mc_ref, x_ref, w_ref, g_ref, wr_ref, x1_ref, h2_ref, aff_ref):
    wa = ma_ref.shape[1]
    subs = [slice(i, i + PROJ_SUB) for i in range(0, x_ref.shape[0], PROJ_SUB)]
    accs = [jnp.dot(ma_ref[r, :], w_ref[:wa], preferred_element_type=F32)
            + jnp.dot(mc_ref[r, :], w_ref[wa:], preferred_element_type=F32) for r in subs]
    for r, acc in zip(subs, accs):
        x1 = x_ref[r, :] + acc
        x1_ref[r, :] = x1
        h = _rms(x1, g_ref[...]).astype(BF16)
        h2_ref[r, :] = h
        aff_ref[:, r] = _router_affinity_t(h, wr_ref[...])


def _out_proj(ma, mc, x2, w_all, g_all, wr_all, l, *, B, S, tm=2 * PROJ_SUB):
    T, D = x2.shape
    Wa = ma.shape[1]
    Wc = mc.shape[1]
    nt = S // tm
    vmem = 2 * (tm * (Wa + Wc) * 2 + 2 * tm * D * 4 + tm * D * 2 + N_EXPERTS * tm * 4) \
        + 2 * (D * D * 2 + 2 * N_EXPERTS * D * 2) + 2 * tm * D * 4 + tm * D * 2 + 4 * MIB
    return pl.pallas_call(
        _out_proj_kernel,
        grid=(T // tm,),
        in_specs=[
            pl.BlockSpec((tm, Wa), lambda i: (i, 0)),
            pl.BlockSpec((tm, Wc), lambda i: (i, 0)),
            pl.BlockSpec((tm, D), lambda i: (i, 0)),
            pl.BlockSpec((None, Wa + Wc, D), lambda i: (l, 0, 0)),
            _layer_vec(l, D),
            pl.BlockSpec((None, 2 * N_EXPERTS, D), lambda i: (l, 0, 0)),
        ],
        out_specs=[
            pl.BlockSpec((tm, D), lambda i: (i, 0)),
            pl.BlockSpec((tm, D), lambda i: (i, 0)),
            pl.BlockSpec((None, N_EXPERTS, tm), lambda i: (i // nt, 0, i % nt)),
        ],
        out_shape=[
            jax.ShapeDtypeStruct((T, D), F32),
            jax.ShapeDtypeStruct((T, D), BF16),
            jax.ShapeDtypeStruct((B, N_EXPERTS, S), F32),
        ],
        compiler_params=_params(vmem, ("arbitrary",)),
        name="out_proj",
    )(ma, mc, x2, w_all, g_all, wr_all)


ROUTE_BLK = 256


def _excl_cumsum_rows(mask_f32, tri):
    S = mask_f32.shape[0]
    carry = jnp.zeros((1, mask_f32.shape[1]), F32)
    out = []
    for i in range(S // ROUTE_BLK):
        blk = mask_f32[i * ROUTE_BLK:(i + 1) * ROUTE_BLK]
        out.append(jnp.dot(tri, blk.astype(BF16), preferred_element_type=F32) + carry)
        carry = carry + jnp.sum(blk, axis=0, keepdims=True)
    return jnp.concatenate(out, axis=0)


def _route_kernel(aff_ref, pos_tok_ref, pos_rows_ref, *, cap):
    B, E, S = aff_ref.shape
    N = B * E
    aff = aff_ref[...].reshape(N, S).T
    bits = pltpu.bitcast(aff, jnp.int32)

    def bisect(_, lh):
        lo, hi = lh
        mid = lo + ((hi - lo + 1) >> 1)
        cnt = jnp.sum((bits >= mid).astype(jnp.int32), axis=0, keepdims=True)
        ok = cnt >= cap
        return jnp.where(ok, mid, lo), jnp.where(ok, hi, mid - 1)

    lo0 = jnp.zeros((1, N), jnp.int32)
    hi0 = jnp.full((1, N), 0x7F800000, jnp.int32)
    thr, _ = lax.fori_loop(0, 31, bisect, (lo0, hi0))

    r = lax.broadcasted_iota(jnp.int32, (ROUTE_BLK, ROUTE_BLK), 0)
    c = lax.broadcasted_iota(jnp.int32, (ROUTE_BLK, ROUTE_BLK), 1)
    tri = jnp.where(r > c, 1.0, 0.0).astype(BF16)

    gt = jnp.where(bits > thr, 1.0, 0.0)
    eq = jnp.where(bits == thr, 1.0, 0.0)
    need = cap - jnp.sum(gt, axis=0, keepdims=True)
    sel = gt + eq * jnp.where(_excl_cumsum_rows(eq, tri) < need, 1.0, 0.0)
    pos = jnp.where(sel > 0.5, _excl_cumsum_rows(sel, tri), -1.0)
    for b in range(B):
        pos_tok_ref[b] = pos[:, b * E:(b + 1) * E].astype(jnp.int32)
    pos_rows_ref[...] = pos.T.astype(jnp.int32)


def _route(aff_t, *, cap):
    B, E, S = aff_t.shape
    return pl.pallas_call(
        functools.partial(_route_kernel, cap=cap),
        out_shape=[jax.ShapeDtypeStruct((B, S, E), jnp.int32),
                   jax.ShapeDtypeStruct((B * E, S), jnp.int32)],
        compiler_params=_params(32 * MIB),
        name="route",
    )(aff_t)


GATHER_EG = 8


def _gather_kernel(pos_ref, aff_ref, h_ref, xs_ref, gate_ref, p_ref, *, cap):
    S = h_ref.shape[0]
    slot = lax.broadcasted_iota(jnp.int32, (cap, S), 0)
    for e in range(GATHER_EG):
        hit = slot == pos_ref[e:e + 1, :]
        p_ref[e * cap:(e + 1) * cap, :] = jnp.where(hit, 1.0, 0.0).astype(BF16)
        g = jnp.sum(jnp.where(hit, aff_ref[0, e:e + 1, :], 0.0), axis=-1, keepdims=True)
        gate_ref[0, e] = jnp.broadcast_to(g, (cap, LANES))
    xs = jnp.dot(p_ref[...], h_ref[...], preferred_element_type=F32)
    xs_ref[0] = xs.astype(BF16).reshape(GATHER_EG, cap, -1)


def _gather(pos_rows, aff_t, h2, *, B, S, E, cap):
    D = h2.shape[1]
    ng = E // GATHER_EG
    vmem = 2 * (2 * GATHER_EG * S * 4 + S * D * 2 + GATHER_EG * cap * (D * 2 + LANES * 4)) \
        + GATHER_EG * cap * S * 2 + GATHER_EG * cap * (S + D) * 4
    return pl.pallas_call(
        functools.partial(_gather_kernel, cap=cap),
        grid=(B, ng),
        in_specs=[
            pl.BlockSpec((GATHER_EG, S), lambda b, j: (b * ng + j, 0)),
            pl.BlockSpec((1, GATHER_EG, S), lambda b, j: (b, j, 0)),
            pl.BlockSpec((S, D), lambda b, j: (b, 0)),
        ],
        out_specs=[
            pl.BlockSpec((1, GATHER_EG, cap, D), lambda b, j: (b, j, 0, 0)),
            pl.BlockSpec((1, GATHER_EG, cap, LANES), lambda b, j: (b, j, 0, 0)),
        ],
        out_shape=[
            jax.ShapeDtypeStruct((B, E, cap, D), BF16),
            jax.ShapeDtypeStruct((B, E, cap, LANES), F32),
        ],
        scratch_shapes=[pltpu.VMEM((GATHER_EG * cap, S), BF16)],
        compiler_params=_params(vmem, ("arbitrary", "arbitrary")),
        name="gather",
    )(pos_rows, aff_t, h2)


FFN_FC = 512
FFN_MC = 512


def _ffn_kernel(xs_ref, gate_ref, wg_ref, wu_ref, wd_ref, y_ref, acc_ref):
    f = pl.program_id(1)
    nB, _, cap, D = xs_ref.shape
    per = FFN_MC // cap
    nm = nB // per

    @pl.when(f == 0)
    def _():
        acc_ref[...] = jnp.zeros_like(acc_ref)

    wg = wg_ref[...].astype(BF16)
    wu = wu_ref[...].astype(BF16)
    wd = wd_ref[...].astype(BF16)

    def rows(m):
        return xs_ref[m * per:(m + 1) * per, 0].reshape(FFN_MC, D)

    def gate_up(m):
        x = rows(m)
        return (jnp.dot(x, wg, preferred_element_type=F32),
                jnp.dot(x, wu, preferred_element_type=F32))

    gu = gate_up(0)
    for m in range(nm):
        nxt = gate_up(m + 1) if m + 1 < nm else None
        g, u = gu
        hid = (g * jax.nn.sigmoid(g) * u).astype(BF16)
        acc_ref[m * FFN_MC:(m + 1) * FFN_MC, :] += jnp.dot(hid, wd, preferred_element_type=F32)
        gu = nxt

    @pl.when(f == pl.num_programs(1) - 1)
    def _():
        for m in range(nm):
            gate = gate_ref[m * per:(m + 1) * per, 0].reshape(FFN_MC, LANES)
            y = acc_ref[m * FFN_MC:(m + 1) * FFN_MC, :] * jnp.tile(gate, (1, D // LANES))
            y_ref[m * per:(m + 1) * per, 0] = y.astype(BF16).reshape(per, cap, D)


def _ffn(xs, gates, wg_all, wu_all, wd_all, l):
    B, E, cap, D = xs.shape
    FF = wg_all.shape[3]
    M = B * cap
    vmem = 4 * M * D * 2 + 2 * M * LANES * 4 + 6 * D * FFN_FC * 4 + 3 * D * FFN_FC * 2 + M * D * 4 \
        + 8 * FFN_MC * FFN_FC * 4 + 2 * FFN_MC * D * 4
    return pl.pallas_call(
        _ffn_kernel,
        grid=(E, FF // FFN_FC),
        in_specs=[
            pl.BlockSpec((B, 1, cap, D), lambda e, f: (0, e, 0, 0)),
            pl.BlockSpec((B, 1, cap, LANES), lambda e, f: (0, e, 0, 0)),
            pl.BlockSpec((None, None, D, FFN_FC), lambda e, f: (l, e, 0, f)),
            pl.BlockSpec((None, None, D, FFN_FC), lambda e, f: (l, e, 0, f)),
            pl.BlockSpec((None, None, FFN_FC, D), lambda e, f: (l, e, f, 0)),
        ],
        out_specs=pl.BlockSpec((B, 1, cap, D), lambda e, f: (0, e, 0, 0)),
        out_shape=jax.ShapeDtypeStruct((B, E, cap, D), BF16),
        scratch_shapes=[pltpu.VMEM((M, D), F32)],
        compiler_params=_params(vmem, ("arbitrary", "arbitrary")),
        name="ffn",
    )(xs, gates, wg_all, wu_all, wd_all)


COMBINE_TT = 512


def _combine_kernel(pos_ref, y_ref, x_ref, fg_ref, o_ref, p_ref, *, cap, final):
    E = pos_ref.shape[2]
    slot = lax.broadcasted_iota(jnp.int32, (COMBINE_TT, cap), 1)
    pos = pos_ref[0]
    for e in range(E):
        hit = slot == pos[:, e:e + 1]
        p_ref[:, e * cap:(e + 1) * cap] = jnp.where(hit, 1.0, 0.0).astype(BF16)
    x2 = x_ref[...] + jnp.dot(p_ref[...], y_ref[0], preferred_element_type=F32)
    o_ref[...] = _rms(x2, fg_ref[...]) if final else x2


def _combine(pos_tok, y, x1, final_g, *, B, S, cap, final):
    E = pos_tok.shape[2]
    D = x1.shape[1]
    nt = S // COMBINE_TT
    vmem = 2 * (COMBINE_TT * LANES * 4 + E * cap * D * 2 + 2 * COMBINE_TT * D * 4) \
        + COMBINE_TT * E * cap * 2 + 2 * COMBINE_TT * D * 4 + 2 * COMBINE_TT * cap * 4
    return pl.pallas_call(
        functools.partial(_combine_kernel, cap=cap, final=final),
        grid=(B, nt),
        in_specs=[
            pl.BlockSpec((1, COMBINE_TT, E), lambda b, t: (b, t, 0)),
            pl.BlockSpec((1, E * cap, D), lambda b, t: (b, 0, 0)),
            pl.BlockSpec((COMBINE_TT, D), lambda b, t: (b * nt + t, 0)),
            pl.BlockSpec((1, D), lambda b, t: (0, 0)),
        ],
        out_specs=pl.BlockSpec((COMBINE_TT, D), lambda b, t: (b * nt + t, 0)),
        out_shape=jax.ShapeDtypeStruct((B * S, D), F32),
        scratch_shapes=[pltpu.VMEM((COMBINE_TT, E * cap), BF16)],
        compiler_params=_params(vmem, ("arbitrary", "arbitrary")),
        name="combine",
    )(pos_tok, y.reshape(B, E * cap, D), x1, final_g.reshape(1, D))


def kernel(x, norm1_g, w_in, rpb, conv_w, conv_b, conv_ln_g, conv_ln_b, attn_out_g,
           conv_out_g, w_out, norm2_g, w_router, w_gate, w_up, w_down, final_g):
    B, S, D = x.shape
    depth = w_in.shape[0]
    E = w_router.shape[2]
    cap = CAP_FACTOR * S // E
    stack = lambda a: a.reshape(depth, 1, a.shape[-1])
    w_in_bf = w_in.astype(BF16)
    w_out_bf = w_out.astype(BF16)
    wr = _router_weights(w_router)
    bias = _attn_bias_tables(rpb, S // GRID_W)
    x2 = x.reshape(B * S, D)
    for l in range(depth):
        qkv, u = _in_proj(x2, stack(norm1_g), w_in_bf, l)
        ma = _attn(qkv, bias, stack(attn_out_g), l, B=B, S=S)
        mc = _conv(u, conv_w, stack(conv_b), stack(conv_ln_g), stack(conv_ln_b),
                   stack(conv_out_g), l, B=B, S=S)
        x1, h2, aff = _out_proj(ma, mc, x2, w_out_bf, stack(norm2_g), wr, l, B=B, S=S)
        pos_tok, pos_rows = _route(aff, cap=cap)
        xs, gates = _gather(pos_rows, aff, h2, B=B, S=S, E=E, cap=cap)
        y = _ffn(xs, gates, w_gate, w_up, w_down, l)
        x2 = _combine(pos_tok, y, x1, final_g, B=B, S=S, cap=cap, final=(l == depth - 1))
    return x2.reshape(B, S, D)
```

```python
import functools

import jax
import jax.numpy as jnp
import numpy as np
from jax import lax
from jax.experimental import pallas as pl
from jax.experimental.pallas import tpu as pltpu

F32 = jnp.float32
BF16 = jnp.bfloat16

GRID_W = 64
HEADS = 8
HEAD_DIM = 64
ATTN_W = HEADS * HEAD_DIM
WIN_ROWS = 8
WIN_COLS = 16
CONV_K = 31
N_EXPERTS = 16
CAP_FACTOR = 2
EPS = 1e-6
LOG2E = 1.4426950408889634

LANES = 128
SUBLANES = 8
MIB = 1024 * 1024
VMEM_CAP = 56 * MIB

NEG_BIG = -1e30


def _params(vmem_bytes, sem=None):
    kw = dict(vmem_limit_bytes=int(min(max(vmem_bytes, 16 * MIB), VMEM_CAP)))
    if sem is not None:
        kw["dimension_semantics"] = sem
    return pltpu.CompilerParams(**kw)


def _rms(x, g):
    ms = jnp.mean(x * x, axis=-1, keepdims=True)
    return x * lax.rsqrt(ms + EPS) * g


def _layer_vec(l, n):
    return pl.BlockSpec((None, 1, n), lambda *_: (l, 0, 0))


PROJ_SUB = 512


def _in_proj_kernel(x_ref, g_ref, w_ref, qkv_ref, u_ref, *, conv_w):
    subs = [slice(i, i + PROJ_SUB) for i in range(0, x_ref.shape[0], PROJ_SUB)]
    hs = [_rms(x_ref[r, :], g_ref[...]).astype(BF16) for r in subs]
    qkvs = [jnp.dot(h, w_ref[:, :3 * ATTN_W], preferred_element_type=F32) for h in hs]
    ags = [jnp.dot(h, w_ref[:, 3 * ATTN_W:], preferred_element_type=F32) for h in hs]
    for r, qkv, ag in zip(subs, qkvs, ags):
        qkv_ref[r, :ATTN_W] = (qkv[:, :ATTN_W] * (HEAD_DIM ** -0.5 * LOG2E)).astype(BF16)
        qkv_ref[r, ATTN_W:] = qkv[:, ATTN_W:].astype(BF16)
        u_ref[r, :] = ag[:, :conv_w] * jax.nn.sigmoid(ag[:, conv_w:])


def _in_proj(x2, g_all, w_all, l, *, tm=2 * PROJ_SUB):
    T, D = x2.shape
    N = w_all.shape[2]
    conv_w = (N - 3 * ATTN_W) // 2
    vmem = 2 * (tm * D * 4 + tm * 3 * ATTN_W * 2 + tm * conv_w * 4) + 2 * D * N * 2 \
        + tm * N * 4 + tm * D * 2 + 4 * MIB
    return pl.pallas_call(
        functools.partial(_in_proj_kernel, conv_w=conv_w),
        grid=(T // tm,),
        in_specs=[
            pl.BlockSpec((tm, D), lambda i: (i, 0)),
            _layer_vec(l, D),
            pl.BlockSpec((None, D, N), lambda i: (l, 0, 0)),
        ],
        out_specs=[
            pl.BlockSpec((tm, 3 * ATTN_W), lambda i: (i, 0)),
            pl.BlockSpec((tm, conv_w), lambda i: (i, 0)),
        ],
        out_shape=[
            jax.ShapeDtypeStruct((T, 3 * ATTN_W), BF16),
            jax.ShapeDtypeStruct((T, conv_w), F32),
        ],
        compiler_params=_params(vmem, ("arbitrary",)),
        name="in_proj",
    )(x2, g_all, w_all)


def _attn_bias_tables(rpb_all, rows):
    kh = min(WIN_ROWS, rows)
    ncol = 2 * WIN_COLS - 1
    nrow = 2 * WIN_ROWS - 1
    qc = np.arange(GRID_W)[:, None]
    kc = np.arange(GRID_W)[None, :]
    col_off = np.clip(kc - qc + (WIN_COLS - 1), 0, ncol - 1)
    col_sel = (col_off[None] == np.arange(ncol)[:, None, None]).astype(np.float32)
    var = np.arange(kh)[:, None]
    kr = np.arange(kh)[None, :]
    row_off = kr - var + (WIN_ROWS - 1)
    row_sel = (row_off[:, :, None] == np.arange(nrow)[None, None, :]).astype(np.float32)
    win_start = np.clip(qc - WIN_COLS // 2, 0, GRID_W - WIN_COLS)
    mask = (kc >= win_start) & (kc < win_start + WIN_COLS)
    L = rpb_all.shape[0]
    rows_picked = jnp.einsum("lhrc,vkr->lvhkc", rpb_all.astype(F32), jnp.asarray(row_sel),
                             precision=lax.Precision.HIGHEST)
    t = jnp.einsum("lvhkc,cqj->lvhqkj", rows_picked, jnp.asarray(col_sel),
                   precision=lax.Precision.HIGHEST)
    t = jnp.where(jnp.asarray(mask)[:, None, :], t * LOG2E, NEG_BIG)
    return t.reshape(L, kh, HEADS // 2, 2 * GRID_W, kh * GRID_W)


def _attn_kernel(qkv_ref, bias_ref, g_ref, o_ref, s0, s1, p0, p1, l0, l1, *, rows, kh):
    nk = kh * GRID_W
    npair = HEADS // 2
    lane = lax.broadcasted_iota(jnp.int32, (GRID_W, LANES), 1)
    first_head = lane < HEAD_DIM
    s_buf, p_buf, l_buf = (s0, s1), (p0, p1), (l0, l1)

    def key_row(r):
        lo, hi = 0, rows - kh
        return min(max(r - kh // 2, lo), hi) if isinstance(r, int) else jnp.clip(r - kh // 2, lo, hi)

    def row_start(r):
        return r * GRID_W if isinstance(r, int) else pl.multiple_of(r * GRID_W, GRID_W)

    def scores(r, slot):
        q0, k0 = row_start(r), row_start(key_row(r))
        for p in range(npair):
            c = p * LANES
            qp = qkv_ref[pl.ds(q0, GRID_W), c:c + LANES].astype(F32)
            qq = jnp.concatenate(
                [jnp.where(first_head, qp, 0.0), jnp.where(first_head, 0.0, qp)], axis=0
            ).astype(BF16)
            kp = qkv_ref[pl.ds(k0, nk), ATTN_W + c:ATTN_W + c + LANES]
            s_buf[slot][p] = lax.dot_general(qq, kp, (((1,), (1,)), ((), ())),
                                             preferred_element_type=F32)

    def softmax(r, slot):
        var = r - key_row(r)
        for p in range(npair):
            s = s_buf[slot][p] + bias_ref[var, p]
            e = jnp.exp2(s - jnp.max(s, axis=-1, keepdims=True))
            l_buf[slot][p] = jnp.broadcast_to(jnp.sum(e, axis=-1, keepdims=True), (2 * GRID_W, LANES))
            p_buf[slot][p] = e.astype(BF16)

    def values(r, slot):
        q0, k0 = row_start(r), row_start(key_row(r))
        outs = []
        for p in range(npair):
            c = 2 * ATTN_W + p * LANES
            vp = qkv_ref[pl.ds(k0, nk), c:c + LANES]
            pv = jnp.dot(p_buf[slot][p], vp, preferred_element_type=F32) / l_buf[slot][p]
            outs.append(jnp.where(first_head, pv[:GRID_W], pv[GRID_W:]))
        o_row = jnp.concatenate(outs, axis=-1)
        o_ref[pl.ds(q0, GRID_W), :] = _rms(o_row, g_ref[...]).astype(BF16)

    scores(0, 0)
    scores(1, 1)
    softmax(0, 0)

    def body(j, carry):
        i = 2 * j
        scores(i, 0)
        softmax(i - 1, 1)
        values(i - 2, 0)
        scores(i + 1, 1)
        softmax(i, 0)
        values(i - 1, 1)
        return carry

    lax.fori_loop(1, rows // 2, body, 0)
    softmax(rows - 1, 1)
    values(rows - 2, 0)
    values(rows - 1, 1)


def _attn(qkv, bias_all, g_all, l, *, B, S):
    rows = S // GRID_W
    kh = min(WIN_ROWS, rows)
    assert rows % 2 == 0 and rows >= 4, "the row pipeline alternates two buffer sets"
    tab = bias_all.shape[1:]
    nk = kh * GRID_W
    npair = HEADS // 2
    ring = 2 * npair * 2 * GRID_W * (nk * 4 + nk * 2 + LANES * 4)
    vmem = 2 * (S * 3 * ATTN_W * 2 + S * ATTN_W * 2 + int(np.prod(tab)) * 4) + ring + 8 * MIB
    buf = lambda n, dt: pltpu.VMEM((npair, 2 * GRID_W, n), dt)
    return pl.pallas_call(
        functools.partial(_attn_kernel, rows=rows, kh=kh),
        grid=(B,),
        in_specs=[
            pl.BlockSpec((S, 3 * ATTN_W), lambda b: (b, 0)),
            pl.BlockSpec((None,) + tab, lambda b: (l, 0, 0, 0, 0)),
            _layer_vec(l, ATTN_W),
        ],
        out_specs=pl.BlockSpec((S, ATTN_W), lambda b: (b, 0)),
        out_shape=jax.ShapeDtypeStruct((B * S, ATTN_W), BF16),
        scratch_shapes=[buf(nk, F32), buf(nk, F32), buf(nk, BF16), buf(nk, BF16),
                        buf(LANES, F32), buf(LANES, F32)],
        compiler_params=_params(vmem, ("arbitrary",)),
        name="attn",
    )(qkv, bias_all, g_all)


CONV_PAD = 16
CONV_TT = 64


def _conv_kernel(u_ref, w_ref, cb_ref, lg_ref, lb_ref, og_ref, o_ref, pad_ref, dw_ref, *, S, C):
    shift = CONV_PAD - CONV_K // 2
    nsub = CONV_TT // SUBLANES
    for c in range(C // LANES):
        sl = slice(c * LANES, (c + 1) * LANES)
        pad_ref[c, 0:CONV_PAD, :] = jnp.zeros((CONV_PAD, LANES), F32)
        pad_ref[c, CONV_PAD + S:, :] = jnp.zeros((CONV_PAD, LANES), F32)
        pad_ref[c, CONV_PAD:CONV_PAD + S, :] = u_ref[:, sl]
        wb = [jnp.broadcast_to(w_ref[k:k + 1, sl], (SUBLANES, LANES)) for k in range(CONV_K)]

        def tile(i, carry, c=c, sl=sl, wb=wb):
            t0 = pl.multiple_of(i * CONV_TT, CONV_TT)
            accs = [jnp.zeros((SUBLANES, LANES), F32) for _ in range(nsub)]
            for off in range(SUBLANES * (nsub - 1) + CONV_K):
                win = pad_ref[c, pl.ds(t0 + shift + off, SUBLANES, stride=1), :]
                for j in range(nsub):
                    k = off - SUBLANES * j
                    if 0 <= k < CONV_K:
                        accs[j] = accs[j] + win * wb[k]
            dw_ref[pl.ds(t0, CONV_TT), sl] = jnp.concatenate(accs, axis=0)
            return carry

        lax.fori_loop(0, S // CONV_TT, tile, 0)

    def norm_tile(i, carry):
        t0 = pl.multiple_of(i * CONV_TT, CONV_TT)
        dw = dw_ref[pl.ds(t0, CONV_TT), :] + cb_ref[...]
        mu = jnp.mean(dw, axis=-1, keepdims=True)
        d = dw - mu
        var = jnp.mean(d * d, axis=-1, keepdims=True)
        y = d * lax.rsqrt(var + EPS) * lg_ref[...] + lb_ref[...]
        z = y * jax.nn.sigmoid(y)
        o_ref[pl.ds(t0, CONV_TT), :] = _rms(z, og_ref[...]).astype(BF16)
        return carry

    lax.fori_loop(0, S // CONV_TT, norm_tile, 0, unroll=4)


def _conv(u, w_all, cb_all, lg_all, lb_all, og_all, l, *, B, S):
    C = u.shape[1]
    vmem = 2 * (S * C * 4 + S * C * 2) + (S + 2 * CONV_PAD) * C * 4 + S * C * 4 + 4 * MIB
    vec = _layer_vec(l, C)
    return pl.pallas_call(
        functools.partial(_conv_kernel, S=S, C=C),
        grid=(B,),
        in_specs=[
            pl.BlockSpec((S, C), lambda b: (b, 0)),
            pl.BlockSpec((None, CONV_K, C), lambda b: (l, 0, 0)),
            vec, vec, vec, vec,
        ],
        out_specs=pl.BlockSpec((S, C), lambda b: (b, 0)),
        out_shape=jax.ShapeDtypeStruct((B * S, C), BF16),
        scratch_shapes=[pltpu.VMEM((C // LANES, S + 2 * CONV_PAD, LANES), F32),
                        pltpu.VMEM((S, C), F32)],
        compiler_params=_params(vmem, ("arbitrary",)),
        name="conv",
    )(u, w_all, cb_all, lg_all, lb_all, og_all)


def _in_proj_conv_kernel(x_ref, xp_ref, xn_ref, g_ref, w_ref, cw_ref, cb_ref, lg_ref, lb_ref, og_ref,
                         qkv_ref, mc_ref, h_ref, pad_ref, dw_ref, *, C, steps_per_seq):
    tm = x_ref.shape[0]
    step = pl.program_id(0) % steps_per_seq
    g = g_ref[...]
    h_ref[0:CONV_PAD, :] = _rms(xp_ref[...], g).astype(BF16)
    h_ref[CONV_PAD:CONV_PAD + tm, :] = _rms(x_ref[...], g).astype(BF16)
    h_ref[CONV_PAD + tm:, :] = _rms(xn_ref[...], g).astype(BF16)
    ag = jnp.dot(h_ref[...], w_ref[:, 3 * ATTN_W:], preferred_element_type=F32)
    u = ag[:, :C] * jax.nn.sigmoid(ag[:, C:])
    row = lax.broadcasted_iota(jnp.int32, (tm + 2 * CONV_PAD, 1), 0)
    first_in = jnp.where(step == 0, CONV_PAD, 0)
    end_in = jnp.where(step == steps_per_seq - 1, tm + CONV_PAD, tm + 2 * CONV_PAD)
    u = jnp.where((row >= first_in) & (row < end_in), u, 0.0)
    for c in range(C // LANES):
        pad_ref[c] = u[:, c * LANES:(c + 1) * LANES]

    h_main = h_ref[CONV_PAD:CONV_PAD + tm, :]
    q = jnp.dot(h_main, w_ref[:, :ATTN_W], preferred_element_type=F32)
    qkv_ref[:, :ATTN_W] = (q * (HEAD_DIM ** -0.5 * LOG2E)).astype(BF16)
    for n0 in range(ATTN_W, 3 * ATTN_W, ATTN_W):
        kv = jnp.dot(h_main, w_ref[:, n0:n0 + ATTN_W], preferred_element_type=F32)
        qkv_ref[:, n0:n0 + ATTN_W] = kv.astype(BF16)
    bits = pltpu.bitcast(kv[0:SUBLANES, 0:LANES], jnp.uint32)
    anchor = pltpu.bitcast(lax.shift_right_logical(lax.shift_right_logical(bits, jnp.uint32(16)),
                                                   jnp.uint32(16)), F32)

    shift = CONV_PAD - CONV_K // 2
    nsub = CONV_TT // SUBLANES
    for c in range(C // LANES):
        sl = slice(c * LANES, (c + 1) * LANES)
        wb = [jnp.broadcast_to(cw_ref[k:k + 1, sl], (SUBLANES, LANES)) for k in range(CONV_K)]
        for t0 in range(0, tm, CONV_TT):
            accs = [jnp.zeros((SUBLANES, LANES), F32) for _ in range(nsub)]
            for off in range(SUBLANES * (nsub - 1) + CONV_K):
                r0 = t0 + shift + off
                win = pad_ref[c, r0:r0 + SUBLANES, :]
                for j in range(nsub):
                    k = off - SUBLANES * j
                    if 0 <= k < CONV_K:
                        accs[j] = accs[j] + win * wb[k]
            dw_ref[t0:t0 + CONV_TT, sl] = jnp.concatenate(accs, axis=0)
    for t0 in range(0, tm, CONV_TT):
        dw = dw_ref[t0:t0 + CONV_TT, :] + cb_ref[...]
        if t0 == 0:
            dw = dw + jnp.tile(anchor, (CONV_TT // SUBLANES, C // LANES))
        mu = jnp.mean(dw, axis=-1, keepdims=True)
        d = dw - mu
        var = jnp.mean(d * d, axis=-1, keepdims=True)
        y = d * lax.rsqrt(var + EPS) * lg_ref[...] + lb_ref[...]
        z = y * jax.nn.sigmoid(y)
        mc_ref[t0:t0 + CONV_TT, :] = _rms(z, og_ref[...]).astype(BF16)


def _in_proj_conv(x2, g_all, w_all, cw_all, cb_all, lg_all, lb_all, og_all, l, *, S, tm=2 * PROJ_SUB):
    T, D = x2.shape
    N = w_all.shape[2]
    C = (N - 3 * ATTN_W) // 2
    nt = S // tm
    hb = tm // CONV_PAD
    last = T // CONV_PAD - 1
    vmem = 2 * (tm * D * 4 + tm * 3 * ATTN_W * 2 + tm * C * 2) + 2 * D * N * 2 \
        + (tm + 2 * CONV_PAD) * (D * 2 + 2 * C * 4 + 2 * C * 4) + tm * 3 * ATTN_W * 4 + tm * C * 4 + 4 * MIB
    vec = _layer_vec(l, C)
    return pl.pallas_call(
        functools.partial(_in_proj_conv_kernel, C=C, steps_per_seq=nt),
        grid=(T // tm,),
        in_specs=[
            pl.BlockSpec((tm, D), lambda i: (i, 0)),
            pl.BlockSpec((CONV_PAD, D), lambda i: (jnp.maximum(i * hb - 1, 0), 0)),
            pl.BlockSpec((CONV_PAD, D), lambda i: (jnp.minimum((i + 1) * hb, last), 0)),
            _layer_vec(l, D),
            pl.BlockSpec((None, D, N), lambda i: (l, 0, 0)),
            pl.BlockSpec((None, CONV_K, C), lambda i: (l, 0, 0)),
            vec, vec, vec, vec,
        ],
        out_specs=[
            pl.BlockSpec((tm, 3 * ATTN_W), lambda i: (i, 0)),
            pl.BlockSpec((tm, C), lambda i: (i, 0)),
        ],
        out_shape=[
            jax.ShapeDtypeStruct((T, 3 * ATTN_W), BF16),
            jax.ShapeDtypeStruct((T, C), BF16),
        ],
        scratch_shapes=[pltpu.VMEM((tm + 2 * CONV_PAD, D), BF16),
                        pltpu.VMEM((C // LANES, tm + 2 * CONV_PAD, LANES), F32),
                        pltpu.VMEM((tm, C), F32)],
        compiler_params=_params(vmem, ("arbitrary",)),
        name="in_proj_conv",
    )(x2, x2, x2, g_all, w_all, cw_all, cb_all, lg_all, lb_all, og_all)


def _router_weights(w_router_all):
    wt = jnp.swapaxes(w_router_all, 1, 2)
    hi = wt.astype(BF16)
    lo = (wt - hi.astype(F32)).astype(BF16)
    return jnp.concatenate([hi, lo], axis=1)


def _router_affinity_t(h_bf16, wr_t):
    lt = lax.dot_general(wr_t, h_bf16, (((1,), (1,)), ((), ())), preferred_element_type=F32)
    logits = lt[:N_EXPERTS] + lt[N_EXPERTS:]
    e = jnp.exp(logits - jnp.max(logits, axis=0, keepdims=True))
    return e / jnp.sum(e, axis=0, keepdims=True)


def _out_proj_kernel(ma_ref, mc_ref, x_ref, w_ref, g_ref, wr_ref, x1_ref, h2_ref, aff_ref):
    wa = ma_ref.shape[1]
    subs = [slice(i, i + PROJ_SUB) for i in range(0, x_ref.shape[0], PROJ_SUB)]
    accs = [jnp.dot(ma_ref[r, :], w_ref[:wa], preferred_element_type=F32)
            + jnp.dot(mc_ref[r, :], w_ref[wa:], preferred_element_type=F32) for r in subs]
    for r, acc in zip(subs, accs):
        x1 = x_ref[r, :] + acc
        x1_ref[r, :] = x1
        h = _rms(x1, g_ref[...]).astype(BF16)
        h2_ref[r, :] = h
        aff_ref[:, r] = _router_affinity_t(h, wr_ref[...])


def _out_proj(ma, mc, x2, w_all, g_all, wr_all, l, *, B, S, tm=2 * PROJ_SUB):
    T, D = x2.shape
    Wa = ma.shape[1]
    Wc = mc.shape[1]
    nt = S // tm
    vmem = 2 * (tm * (Wa + Wc) * 2 + 2 * tm * D * 4 + tm * D * 2 + N_EXPERTS * tm * 4) \
        + 2 * (D * D * 2 + 2 * N_EXPERTS * D * 2) + 2 * tm * D * 4 + tm * D * 2 + 4 * MIB
    return pl.pallas_call(
        _out_proj_kernel,
        grid=(T // tm,),
        in_specs=[
            pl.BlockSpec((tm, Wa), lambda i: (i, 0)),
            pl.BlockSpec((tm, Wc), lambda i: (i, 0)),
            pl.BlockSpec((tm, D), lambda i: (i, 0)),
            pl.BlockSpec((None, Wa + Wc, D), lambda i: (l, 0, 0)),
            _layer_vec(l, D),
            pl.BlockSpec((None, 2 * N_EXPERTS, D), lambda i: (l, 0, 0)),
        ],
        out_specs=[
            pl.BlockSpec((tm, D), lambda i: (i, 0)),
            pl.BlockSpec((tm, D), lambda i: (i, 0)),
            pl.BlockSpec((None, N_EXPERTS, tm), lambda i: (i // nt, 0, i % nt)),
        ],
        out_shape=[
            jax.ShapeDtypeStruct((T, D), F32),
            jax.ShapeDtypeStruct((T, D), BF16),
            jax.ShapeDtypeStruct((B, N_EXPERTS, S), F32),
        ],
        compiler_params=_params(vmem, ("arbitrary",)),
        name="out_proj",
    )(ma, mc, x2, w_all, g_all, wr_all)


ROUTE_BLK = 256


def _excl_cumsum_rows(mask_f32, tri):
    S = mask_f32.shape[0]
    carry = jnp.zeros((1, mask_f32.shape[1]), F32)
    out = []
    for i in range(S // ROUTE_BLK):
        blk = mask_f32[i * ROUTE_BLK:(i + 1) * ROUTE_BLK]
        out.append(jnp.dot(tri, blk.astype(BF16), preferred_element_type=F32) + carry)
        carry = carry + jnp.sum(blk, axis=0, keepdims=True)
    return jnp.concatenate(out, axis=0)


def _route_kernel(aff_ref, pos_tok_ref, pos_rows_ref, *, cap):
    B, E, S = aff_ref.shape
    N = B * E
    aff = aff_ref[...].reshape(N, S).T
    bits = pltpu.bitcast(aff, jnp.int32)

    def bisect(_, lh):
        lo, hi = lh
        mid = lo + ((hi - lo + 1) >> 1)
        cnt = jnp.sum((bits >= mid).astype(jnp.int32), axis=0, keepdims=True)
        ok = cnt >= cap
        return jnp.where(ok, mid, lo), jnp.where(ok, hi, mid - 1)

    lo0 = jnp.zeros((1, N), jnp.int32)
    hi0 = jnp.full((1, N), 0x7F800000, jnp.int32)
    thr, _ = lax.fori_loop(0, 31, bisect, (lo0, hi0))

    r = lax.broadcasted_iota(jnp.int32, (ROUTE_BLK, ROUTE_BLK), 0)
    c = lax.broadcasted_iota(jnp.int32, (ROUTE_BLK, ROUTE_BLK), 1)
    tri = jnp.where(r > c, 1.0, 0.0).astype(BF16)

    gt = jnp.where(bits > thr, 1.0, 0.0)
    eq = jnp.where(bits == thr, 1.0, 0.0)
    need = cap - jnp.sum(gt, axis=0, keepdims=True)
    sel = gt + eq * jnp.where(_excl_cumsum_rows(eq, tri) < need, 1.0, 0.0)
    pos = jnp.where(sel > 0.5, _excl_cumsum_rows(sel, tri), -1.0)
    for b in range(B):
        pos_tok_ref[b] = pos[:, b * E:(b + 1) * E].astype(jnp.int32)
    pos_rows_ref[...] = pos.T.astype(jnp.int32)


def _route(aff_t, *, cap):
    B, E, S = aff_t.shape
    return pl.pallas_call(
        functools.partial(_route_kernel, cap=cap),
        out_shape=[jax.ShapeDtypeStruct((B, S, E), jnp.int32),
                   jax.ShapeDtypeStruct((B * E, S), jnp.int32)],
        compiler_params=_params(32 * MIB),
        name="route",
    )(aff_t)


GATHER_EG = 8


def _gather_kernel(pos_ref, aff_ref, h_ref, xs_ref, gate_ref, p_ref, *, cap):
    S = h_ref.shape[0]
    slot = lax.broadcasted_iota(jnp.int32, (cap, S), 0)
    for e in range(GATHER_EG):
        hit = slot == pos_ref[e:e + 1, :]
        p_ref[e * cap:(e + 1) * cap, :] = jnp.where(hit, 1.0, 0.0).astype(BF16)
        g = jnp.sum(jnp.where(hit, aff_ref[0, e:e + 1, :], 0.0), axis=-1, keepdims=True)
        gate_ref[0, e] = jnp.broadcast_to(g, (cap, LANES))
    xs = jnp.dot(p_ref[...], h_ref[...], preferred_element_type=F32)
    xs_ref[0] = xs.astype(BF16).reshape(GATHER_EG, cap, -1)


def _gather(pos_rows, aff_t, h2, *, B, S, E, cap):
    D = h2.shape[1]
    ng = E // GATHER_EG
    vmem = 2 * (2 * GATHER_EG * S * 4 + S * D * 2 + GATHER_EG * cap * (D * 2 + LANES * 4)) \
        + GATHER_EG * cap * S * 2 + GATHER_EG * cap * (S + D) * 4
    return pl.pallas_call(
        functools.partial(_gather_kernel, cap=cap),
        grid=(B, ng),
        in_specs=[
            pl.BlockSpec((GATHER_EG, S), lambda b, j: (b * ng + j, 0)),
            pl.BlockSpec((1, GATHER_EG, S), lambda b, j: (b, j, 0)),
            pl.BlockSpec((S, D), lambda b, j: (b, 0)),
        ],
        out_specs=[
            pl.BlockSpec((1, GATHER_EG, cap, D), lambda b, j: (b, j, 0, 0)),
            pl.BlockSpec((1, GATHER_EG, cap, LANES), lambda b, j: (b, j, 0, 0)),
        ],
        out_shape=[
            jax.ShapeDtypeStruct((B, E, cap, D), BF16),
            jax.ShapeDtypeStruct((B, E, cap, LANES), F32),
        ],
        scratch_shapes=[pltpu.VMEM((GATHER_EG * cap, S), BF16)],
        compiler_params=_params(vmem, ("arbitrary", "arbitrary")),
        name="gather",
    )(pos_rows, aff_t, h2)


FFN_FC = 512
FFN_MC = 512


def _ffn_kernel(xs_ref, gate_ref, wg_ref, wu_ref, wd_ref, y_ref, acc_ref):
    f = pl.program_id(1)
    nB, _, cap, D = xs_ref.shape
    per = FFN_MC // cap
    nm = nB // per

    @pl.when(f == 0)
    def _():
        acc_ref[...] = jnp.zeros_like(acc_ref)

    wg = wg_ref[...].astype(BF16)
    wu = wu_ref[...].astype(BF16)
    wd = wd_ref[...].astype(BF16)

    def rows(m):
        return xs_ref[m * per:(m + 1) * per, 0].reshape(FFN_MC, D)

    def gate_up(m):
        x = rows(m)
        return (jnp.dot(x, wg, preferred_element_type=F32),
                jnp.dot(x, wu, preferred_element_type=F32))

    gu = gate_up(0)
    for m in range(nm):
        nxt = gate_up(m + 1) if m + 1 < nm else None
        g, u = gu
        hid = (g * jax.nn.sigmoid(g) * u).astype(BF16)
        acc_ref[m * FFN_MC:(m + 1) * FFN_MC, :] += jnp.dot(hid, wd, preferred_element_type=F32)
        gu = nxt

    @pl.when(f == pl.num_programs(1) - 1)
    def _():
        for m in range(nm):
            gate = gate_ref[m * per:(m + 1) * per, 0].reshape(FFN_MC, LANES)
            y = acc_ref[m * FFN_MC:(m + 1) * FFN_MC, :] * jnp.tile(gate, (1, D // LANES))
            y_ref[m * per:(m + 1) * per, 0] = y.astype(BF16).reshape(per, cap, D)


def _ffn(xs, gates, wg_all, wu_all, wd_all, l):
    B, E, cap, D = xs.shape
    FF = wg_all.shape[3]
    M = B * cap
    vmem = 4 * M * D * 2 + 2 * M * LANES * 4 + 6 * D * FFN_FC * 4 + 3 * D * FFN_FC * 2 + M * D * 4 \
        + 8 * FFN_MC * FFN_FC * 4 + 2 * FFN_MC * D * 4
    return pl.pallas_call(
        _ffn_kernel,
        grid=(E, FF // FFN_FC),
        in_specs=[
            pl.BlockSpec((B, 1, cap, D), lambda e, f: (0, e, 0, 0)),
            pl.BlockSpec((B, 1, cap, LANES), lambda e, f: (0, e, 0, 0)),
            pl.BlockSpec((None, None, D, FFN_FC), lambda e, f: (l, e, 0, f)),
            pl.BlockSpec((None, None, D, FFN_FC), lambda e, f: (l, e, 0, f)),
            pl.BlockSpec((None, None, FFN_FC, D), lambda e, f: (l, e, f, 0)),
        ],
        out_specs=pl.BlockSpec((B, 1, cap, D), lambda e, f: (0, e, 0, 0)),
        out_shape=jax.ShapeDtypeStruct((B, E, cap, D), BF16),
        scratch_shapes=[pltpu.VMEM((M, D), F32)],
        compiler_params=_params(vmem, ("arbitrary", "arbitrary")),
        name="ffn",
    )(xs, gates, wg_all, wu_all, wd_all)


COMBINE_TT = 512


def _combine_kernel(pos_ref, y_ref, x_ref, fg_ref, o_ref, p_ref, *, cap, final):
    E = pos_ref.shape[2]
    slot = lax.broadcasted_iota(jnp.int32, (COMBINE_TT, cap), 1)
    pos = pos_ref[0]
    for e in range(E):
        hit = slot == pos[:, e:e + 1]
        p_ref[:, e * cap:(e + 1) * cap] = jnp.where(hit, 1.0, 0.0).astype(BF16)
    x2 = x_ref[...] + jnp.dot(p_ref[...], y_ref[0], preferred_element_type=F32)
    o_ref[...] = _rms(x2, fg_ref[...]) if final else x2


def _combine(pos_tok, y, x1, final_g, *, B, S, cap, final):
    E = pos_tok.shape[2]
    D = x1.shape[1]
    nt = S // COMBINE_TT
    vmem = 2 * (COMBINE_TT * LANES * 4 + E * cap * D * 2 + 2 * COMBINE_TT * D * 4) \
        + COMBINE_TT * E * cap * 2 + 2 * COMBINE_TT * D * 4 + 2 * COMBINE_TT * cap * 4
    return pl.pallas_call(
        functools.partial(_combine_kernel, cap=cap, final=final),
        grid=(B, nt),
        in_specs=[
            pl.BlockSpec((1, COMBINE_TT, E), lambda b, t: (b, t, 0)),
            pl.BlockSpec((1, E * cap, D), lambda b, t: (b, 0, 0)),
            pl.BlockSpec((COMBINE_TT, D), lambda b, t: (b * nt + t, 0)),
            pl.BlockSpec((1, D), lambda b, t: (0, 0)),
        ],
        out_specs=pl.BlockSpec((COMBINE_TT, D), lambda b, t: (b * nt + t, 0)),
        out_shape=jax.ShapeDtypeStruct((B * S, D), F32),
        scratch_shapes=[pltpu.VMEM((COMBINE_TT, E * cap), BF16)],
        compiler_params=_params(vmem, ("arbitrary", "arbitrary")),
        name="combine",
    )(pos_tok, y.reshape(B, E * cap, D), x1, final_g.reshape(1, D))


def kernel(x, norm1_g, w_in, rpb, conv_w, conv_b, conv_ln_g, conv_ln_b, attn_out_g,
           conv_out_g, w_out, norm2_g, w_router, w_gate, w_up, w_down, final_g):
    B, S, D = x.shape
    depth = w_in.shape[0]
    E = w_router.shape[2]
    cap = CAP_FACTOR * S // E
    stack = lambda a: a.reshape(depth, 1, a.shape[-1])
    w_in_bf = w_in.astype(BF16)
    w_out_bf = w_out.astype(BF16)
    wr = _router_weights(w_router)
    bias = _attn_bias_tables(rpb, S // GRID_W)
    x2 = x.reshape(B * S, D)
    for l in range(depth):
        qkv, mc = _in_proj_conv(x2, stack(norm1_g), w_in_bf, conv_w, stack(conv_b), stack(conv_ln_g),
                                stack(conv_ln_b), stack(conv_out_g), l, S=S)
        ma = _attn(qkv, bias, stack(attn_out_g), l, B=B, S=S)
        x1, h2, aff = _out_proj(ma, mc, x2, w_out_bf, stack(norm2_g), wr, l, B=B, S=S)
        pos_tok, pos_rows = _route(aff, cap=cap)
        xs, gates = _gather(pos_rows, aff, h2, B=B, S=S, E=E, cap=cap)
        y = _ffn(xs, gates, w_gate, w_up, w_down, l)
        x2 = _combine(pos_tok, y, x1, final_g, B=B, S=S, cap=cap, final=(l == depth - 1))
    return x2.reshape(B, S, D)
```

```python
import functools

import jax
import jax.numpy as jnp
import numpy as np
from jax import lax
from jax.experimental import pallas as pl
from jax.experimental.pallas import tpu as pltpu

F32 = jnp.float32
BF16 = jnp.bfloat16

GRID_W = 64
HEADS = 8
HEAD_DIM = 64
ATTN_W = HEADS * HEAD_DIM
WIN_ROWS = 8
WIN_COLS = 16
CONV_K = 31
N_EXPERTS = 16
CAP_FACTOR = 2
EPS = 1e-6
LOG2E = 1.4426950408889634

LANES = 128
SUBLANES = 8
MIB = 1024 * 1024
VMEM_CAP = 56 * MIB

NEG_BIG = -1e30


def _params(vmem_bytes, sem=None):
    kw = dict(vmem_limit_bytes=int(min(max(vmem_bytes, 16 * MIB), VMEM_CAP)))
    if sem is not None:
        kw["dimension_semantics"] = sem
    return pltpu.CompilerParams(**kw)


def _rms(x, g):
    ms = jnp.mean(x * x, axis=-1, keepdims=True)
    return x * lax.rsqrt(ms + EPS) * g


def _layer_vec(l, n):
    return pl.BlockSpec((None, 1, n), lambda *_: (l, 0, 0))


PROJ_SUB = 512


def _attn_bias_tables(rpb_all, rows):
    kh = min(WIN_ROWS, rows)
    ncol = 2 * WIN_COLS - 1
    nrow = 2 * WIN_ROWS - 1
    qc = np.arange(GRID_W)[:, None]
    kc = np.arange(GRID_W)[None, :]
    col_off = np.clip(kc - qc + (WIN_COLS - 1), 0, ncol - 1)
    col_sel = (col_off[None] == np.arange(ncol)[:, None, None]).astype(np.float32)
    var = np.arange(kh)[:, None]
    kr = np.arange(kh)[None, :]
    row_off = kr - var + (WIN_ROWS - 1)
    row_sel = (row_off[:, :, None] == np.arange(nrow)[None, None, :]).astype(np.float32)
    win_start = np.clip(qc - WIN_COLS // 2, 0, GRID_W - WIN_COLS)
    mask = (kc >= win_start) & (kc < win_start + WIN_COLS)
    L = rpb_all.shape[0]
    rows_picked = jnp.einsum("lhrc,vkr->lvhkc", rpb_all.astype(F32), jnp.asarray(row_sel),
                             precision=lax.Precision.HIGHEST)
    t = jnp.einsum("lvhkc,cqj->lvhqkj", rows_picked, jnp.asarray(col_sel),
                   precision=lax.Precision.HIGHEST)
    t = jnp.where(jnp.asarray(mask)[:, None, :], t * LOG2E, NEG_BIG)
    return t.reshape(L, kh, HEADS // 2, 2 * GRID_W, kh * GRID_W)


def _attn_kernel(qkv_ref, bias_ref, g_ref, o_ref, s0, s1, p0, p1, l0, l1, *, rows, kh):
    nk = kh * GRID_W
    npair = HEADS // 2
    lane = lax.broadcasted_iota(jnp.int32, (GRID_W, LANES), 1)
    first_head = lane < HEAD_DIM
    s_buf, p_buf, l_buf = (s0, s1), (p0, p1), (l0, l1)

    def key_row(r):
        lo, hi = 0, rows - kh
        return min(max(r - kh // 2, lo), hi) if isinstance(r, int) else jnp.clip(r - kh // 2, lo, hi)

    def row_start(r):
        return r * GRID_W if isinstance(r, int) else pl.multiple_of(r * GRID_W, GRID_W)

    def scores(r, slot):
        q0, k0 = row_start(r), row_start(key_row(r))
        for p in range(npair):
            c = p * LANES
            qp = qkv_ref[pl.ds(q0, GRID_W), c:c + LANES].astype(F32)
            qq = jnp.concatenate(
                [jnp.where(first_head, qp, 0.0), jnp.where(first_head, 0.0, qp)], axis=0
            ).astype(BF16)
            kp = qkv_ref[pl.ds(k0, nk), ATTN_W + c:ATTN_W + c + LANES]
            s_buf[slot][p] = lax.dot_general(qq, kp, (((1,), (1,)), ((), ())),
                                             preferred_element_type=F32)

    def softmax(r, slot):
        var = r - key_row(r)
        for p in range(npair):
            s = s_buf[slot][p] + bias_ref[var, p]
            e = jnp.exp2(s - jnp.max(s, axis=-1, keepdims=True))
            l_buf[slot][p] = jnp.broadcast_to(jnp.sum(e, axis=-1, keepdims=True), (2 * GRID_W, LANES))
            p_buf[slot][p] = e.astype(BF16)

    def values(r, slot):
        q0, k0 = row_start(r), row_start(key_row(r))
        outs = []
        for p in range(npair):
            c = 2 * ATTN_W + p * LANES
            vp = qkv_ref[pl.ds(k0, nk), c:c + LANES]
            pv = jnp.dot(p_buf[slot][p], vp, preferred_element_type=F32) / l_buf[slot][p]
            outs.append(jnp.where(first_head, pv[:GRID_W], pv[GRID_W:]))
        o_row = jnp.concatenate(outs, axis=-1)
        o_ref[pl.ds(q0, GRID_W), :] = _rms(o_row, g_ref[...]).astype(BF16)

    scores(0, 0)
    scores(1, 1)
    softmax(0, 0)

    def body(j, carry):
        i = 2 * j
        scores(i, 0)
        softmax(i - 1, 1)
        values(i - 2, 0)
        scores(i + 1, 1)
        softmax(i, 0)
        values(i - 1, 1)
        return carry

    lax.fori_loop(1, rows // 2, body, 0)
    softmax(rows - 1, 1)
    values(rows - 2, 0)
    values(rows - 1, 1)


def _attn(qkv, bias_all, g_all, l, *, B, S):
    rows = S // GRID_W
    kh = min(WIN_ROWS, rows)
    assert rows % 2 == 0 and rows >= 4, "the row pipeline alternates two buffer sets"
    tab = bias_all.shape[1:]
    nk = kh * GRID_W
    npair = HEADS // 2
    ring = 2 * npair * 2 * GRID_W * (nk * 4 + nk * 2 + LANES * 4)
    vmem = 2 * (S * 3 * ATTN_W * 2 + S * ATTN_W * 2 + int(np.prod(tab)) * 4) + ring + 8 * MIB
    buf = lambda n, dt: pltpu.VMEM((npair, 2 * GRID_W, n), dt)
    return pl.pallas_call(
        functools.partial(_attn_kernel, rows=rows, kh=kh),
        grid=(B,),
        in_specs=[
            pl.BlockSpec((S, 3 * ATTN_W), lambda b: (b, 0)),
            pl.BlockSpec((None,) + tab, lambda b: (l, 0, 0, 0, 0)),
            _layer_vec(l, ATTN_W),
        ],
        out_specs=pl.BlockSpec((S, ATTN_W), lambda b: (b, 0)),
        out_shape=jax.ShapeDtypeStruct((B * S, ATTN_W), BF16),
        scratch_shapes=[buf(nk, F32), buf(nk, F32), buf(nk, BF16), buf(nk, BF16),
                        buf(LANES, F32), buf(LANES, F32)],
        compiler_params=_params(vmem, ("arbitrary",)),
        name="attn",
    )(qkv, bias_all, g_all)


CONV_PAD = 16
CONV_TT = 64
def _in_proj_conv_kernel(x_ref, xp_ref, xn_ref, g_ref, w_ref, cw_ref, cb_ref, lg_ref, lb_ref, og_ref,
                         qkv_ref, mc_ref, h_ref, pad_ref, dw_ref, *, C, steps_per_seq):
    tm = x_ref.shape[0]
    step = pl.program_id(0) % steps_per_seq
    g = g_ref[...]
    h_ref[0:CONV_PAD, :] = _rms(xp_ref[...], g).astype(BF16)
    h_ref[CONV_PAD:CONV_PAD + tm, :] = _rms(x_ref[...], g).astype(BF16)
    h_ref[CONV_PAD + tm:, :] = _rms(xn_ref[...], g).astype(BF16)
    ag = jnp.dot(h_ref[...], w_ref[:, 3 * ATTN_W:], preferred_element_type=F32)
    u = ag[:, :C] * jax.nn.sigmoid(ag[:, C:])
    row = lax.broadcasted_iota(jnp.int32, (tm + 2 * CONV_PAD, 1), 0)
    first_in = jnp.where(step == 0, CONV_PAD, 0)
    end_in = jnp.where(step == steps_per_seq - 1, tm + CONV_PAD, tm + 2 * CONV_PAD)
    u = jnp.where((row >= first_in) & (row < end_in), u, 0.0)
    for c in range(C // LANES):
        pad_ref[c] = u[:, c * LANES:(c + 1) * LANES]

    h_main = h_ref[CONV_PAD:CONV_PAD + tm, :]
    anchors = {}
    n_ln = tm // CONV_TT
    for i, n0 in enumerate(range(0, 3 * ATTN_W, ATTN_W)):
        r = jnp.dot(h_main, w_ref[:, n0:n0 + ATTN_W], preferred_element_type=F32)
        if n0 == 0:
            r = r * (HEAD_DIM ** -0.5 * LOG2E)
        qkv_ref[:, n0:n0 + ATTN_W] = r.astype(BF16)
        bits = pltpu.bitcast(r[0:SUBLANES, 0:LANES], jnp.uint32)
        sixteen = jnp.uint32(16)
        zero = pltpu.bitcast(lax.shift_right_logical(lax.shift_right_logical(bits, sixteen), sixteen), F32)
        anchors[(i + 1) * n_ln // 4 - 1] = zero

    shift = CONV_PAD - CONV_K // 2
    nsub = CONV_TT // SUBLANES
    for c in range(C // LANES):
        sl = slice(c * LANES, (c + 1) * LANES)
        wb = [jnp.broadcast_to(cw_ref[k:k + 1, sl], (SUBLANES, LANES)) for k in range(CONV_K)]
        for t0 in range(0, tm, CONV_TT):
            accs = [jnp.zeros((SUBLANES, LANES), F32) for _ in range(nsub)]
            for off in range(SUBLANES * (nsub - 1) + CONV_K):
                r0 = t0 + shift + off
                win = pad_ref[c, r0:r0 + SUBLANES, :]
                for j in range(nsub):
                    k = off - SUBLANES * j
                    if 0 <= k < CONV_K:
                        accs[j] = accs[j] + win * wb[k]
            dw_ref[t0:t0 + CONV_TT, sl] = jnp.concatenate(accs, axis=0)
    for t0 in range(0, tm, CONV_TT):
        dw = dw_ref[t0:t0 + CONV_TT, :] + cb_ref[...]
        if t0 // CONV_TT in anchors:
            dw = dw + jnp.tile(anchors[t0 // CONV_TT], (CONV_TT // SUBLANES, C // LANES))
        mu = jnp.mean(dw, axis=-1, keepdims=True)
        d = dw - mu
        var = jnp.mean(d * d, axis=-1, keepdims=True)
        y = d * lax.rsqrt(var + EPS) * lg_ref[...] + lb_ref[...]
        z = y * jax.nn.sigmoid(y)
        mc_ref[t0:t0 + CONV_TT, :] = _rms(z, og_ref[...]).astype(BF16)


def _in_proj_conv(x2, g_all, w_all, cw_all, cb_all, lg_all, lb_all, og_all, l, *, S, tm=2 * PROJ_SUB):
    T, D = x2.shape
    N = w_all.shape[2]
    C = (N - 3 * ATTN_W) // 2
    nt = S // tm
    hb = tm // CONV_PAD
    last = T // CONV_PAD - 1
    vmem = 2 * (tm * D * 4 + tm * 3 * ATTN_W * 2 + tm * C * 2) + 2 * D * N * 2 \
        + (tm + 2 * CONV_PAD) * (D * 2 + 2 * C * 4 + 2 * C * 4) + tm * 3 * ATTN_W * 4 + tm * C * 4 + 4 * MIB
    vec = _layer_vec(l, C)
    return pl.pallas_call(
        functools.partial(_in_proj_conv_kernel, C=C, steps_per_seq=nt),
        grid=(T // tm,),
        in_specs=[
            pl.BlockSpec((tm, D), lambda i: (i, 0)),
            pl.BlockSpec((CONV_PAD, D), lambda i: (jnp.maximum(i * hb - 1, 0), 0)),
            pl.BlockSpec((CONV_PAD, D), lambda i: (jnp.minimum((i + 1) * hb, last), 0)),
            _layer_vec(l, D),
            pl.BlockSpec((None, D, N), lambda i: (l, 0, 0)),
            pl.BlockSpec((None, CONV_K, C), lambda i: (l, 0, 0)),
            vec, vec, vec, vec,
        ],
        out_specs=[
            pl.BlockSpec((tm, 3 * ATTN_W), lambda i: (i, 0)),
            pl.BlockSpec((tm, C), lambda i: (i, 0)),
        ],
        out_shape=[
            jax.ShapeDtypeStruct((T, 3 * ATTN_W), BF16),
            jax.ShapeDtypeStruct((T, C), BF16),
        ],
        scratch_shapes=[pltpu.VMEM((tm + 2 * CONV_PAD, D), BF16),
                        pltpu.VMEM((C // LANES, tm + 2 * CONV_PAD, LANES), F32),
                        pltpu.VMEM((tm, C), F32)],
        compiler_params=_params(vmem, ("arbitrary",)),
        name="in_proj_conv",
    )(x2, x2, x2, g_all, w_all, cw_all, cb_all, lg_all, lb_all, og_all)


def _router_weights(w_router_all):
    wt = jnp.swapaxes(w_router_all, 1, 2)
    hi = wt.astype(BF16)
    lo = (wt - hi.astype(F32)).astype(BF16)
    return jnp.concatenate([hi, lo], axis=1)


def _router_affinity_t(h_bf16, wr_t):
    lt = lax.dot_general(wr_t, h_bf16, (((1,), (1,)), ((), ())), preferred_element_type=F32)
    logits = lt[:N_EXPERTS] + lt[N_EXPERTS:]
    e = jnp.exp(logits - jnp.max(logits, axis=0, keepdims=True))
    return e / jnp.sum(e, axis=0, keepdims=True)


def _out_proj_kernel(ma_ref, mc_ref, x_ref, w_ref, g_ref, wr_ref, x1_ref, h2_ref, aff_ref):
    wa = ma_ref.shape[1]
    subs = [slice(i, i + PROJ_SUB) for i in range(0, x_ref.shape[0], PROJ_SUB)]
    w_attn = w_ref[:wa].astype(BF16)
    w_conv = w_ref[wa:].astype(BF16)
    accs = [jnp.dot(ma_ref[r, :], w_attn, preferred_element_type=F32)
            + jnp.dot(mc_ref[r, :], w_conv, preferred_element_type=F32) for r in subs]
    for r, acc in zip(subs, accs):
        x1 = x_ref[r, :] + acc
        x1_ref[r, :] = x1
        h = _rms(x1, g_ref[...]).astype(BF16)
        h2_ref[r, :] = h
        aff_ref[:, r] = _router_affinity_t(h, wr_ref[...])


def _out_proj(ma, mc, x2, w_all, g_all, wr_all, l, *, B, S, tm=2 * PROJ_SUB):
    T, D = x2.shape
    Wa = ma.shape[1]
    Wc = mc.shape[1]
    nt = S // tm
    vmem = 2 * (tm * (Wa + Wc) * 2 + 2 * tm * D * 4 + tm * D * 2 + N_EXPERTS * tm * 4) \
        + 2 * (D * D * 4 + 2 * N_EXPERTS * D * 2) + D * D * 2 + 2 * tm * D * 4 + tm * D * 2 + 4 * MIB
    return pl.pallas_call(
        _out_proj_kernel,
        grid=(T // tm,),
        in_specs=[
            pl.BlockSpec((tm, Wa), lambda i: (i, 0)),
            pl.BlockSpec((tm, Wc), lambda i: (i, 0)),
            pl.BlockSpec((tm, D), lambda i: (i, 0)),
            pl.BlockSpec((None, Wa + Wc, D), lambda i: (l, 0, 0)),
            _layer_vec(l, D),
            pl.BlockSpec((None, 2 * N_EXPERTS, D), lambda i: (l, 0, 0)),
        ],
        out_specs=[
            pl.BlockSpec((tm, D), lambda i: (i, 0)),
            pl.BlockSpec((tm, D), lambda i: (i, 0)),
            pl.BlockSpec((None, N_EXPERTS, tm), lambda i: (i // nt, 0, i % nt)),
        ],
        out_shape=[
            jax.ShapeDtypeStruct((T, D), F32),
            jax.ShapeDtypeStruct((T, D), BF16),
            jax.ShapeDtypeStruct((B, N_EXPERTS, S), F32),
        ],
        compiler_params=_params(vmem, ("arbitrary",)),
        name="out_proj",
    )(ma, mc, x2, w_all, g_all, wr_all)


ROUTE_BLK = 256


def _excl_cumsum_rows(mask_f32, tri):
    S = mask_f32.shape[0]
    carry = jnp.zeros((1, mask_f32.shape[1]), F32)
    out = []
    for i in range(S // ROUTE_BLK):
        blk = mask_f32[i * ROUTE_BLK:(i + 1) * ROUTE_BLK]
        out.append(jnp.dot(tri, blk.astype(BF16), preferred_element_type=F32) + carry)
        carry = carry + jnp.sum(blk, axis=0, keepdims=True)
    return jnp.concatenate(out, axis=0)


def _route_kernel(aff_ref, pos_tok_ref, pos_rows_ref, *, cap):
    B, E, S = aff_ref.shape
    N = B * E
    aff = aff_ref[...].reshape(N, S).T
    bits = pltpu.bitcast(aff, jnp.int32)

    def bisect(_, lh):
        lo, hi = lh
        mid = lo + ((hi - lo + 1) >> 1)
        cnt = jnp.sum((bits >= mid).astype(jnp.int32), axis=0, keepdims=True)
        ok = cnt >= cap
        return jnp.where(ok, mid, lo), jnp.where(ok, hi, mid - 1)

    lo0 = jnp.zeros((1, N), jnp.int32)
    hi0 = jnp.full((1, N), 0x7F800000, jnp.int32)
    thr, _ = lax.fori_loop(0, 31, bisect, (lo0, hi0))

    r = lax.broadcasted_iota(jnp.int32, (ROUTE_BLK, ROUTE_BLK), 0)
    c = lax.broadcasted_iota(jnp.int32, (ROUTE_BLK, ROUTE_BLK), 1)
    tri = jnp.where(r > c, 1.0, 0.0).astype(BF16)

    gt = jnp.where(bits > thr, 1.0, 0.0)
    eq = jnp.where(bits == thr, 1.0, 0.0)
    need = cap - jnp.sum(gt, axis=0, keepdims=True)
    sel = gt + eq * jnp.where(_excl_cumsum_rows(eq, tri) < need, 1.0, 0.0)
    pos = jnp.where(sel > 0.5, _excl_cumsum_rows(sel, tri), -1.0)
    for b in range(B):
        pos_tok_ref[b] = pos[:, b * E:(b + 1) * E].astype(jnp.int32)
    pos_rows_ref[...] = pos.T.astype(jnp.int32)


def _route(aff_t, *, cap):
    B, E, S = aff_t.shape
    return pl.pallas_call(
        functools.partial(_route_kernel, cap=cap),
        out_shape=[jax.ShapeDtypeStruct((B, S, E), jnp.int32),
                   jax.ShapeDtypeStruct((B * E, S), jnp.int32)],
        compiler_params=_params(32 * MIB),
        name="route",
    )(aff_t)


GATHER_EG = 8


def _gather_kernel(pos_ref, aff_ref, h_ref, xs_ref, gate_ref, p_ref, *, cap):
    S = h_ref.shape[0]
    slot = lax.broadcasted_iota(jnp.int32, (cap, S), 0)
    for e in range(GATHER_EG):
        hit = slot == pos_ref[e:e + 1, :]
        p_ref[e * cap:(e + 1) * cap, :] = jnp.where(hit, 1.0, 0.0).astype(BF16)
        g = jnp.sum(jnp.where(hit, aff_ref[0, e:e + 1, :], 0.0), axis=-1, keepdims=True)
        gate_ref[0, e] = jnp.broadcast_to(g, (cap, LANES))
    xs = jnp.dot(p_ref[...], h_ref[...], preferred_element_type=F32)
    xs_ref[0] = xs.astype(BF16).reshape(GATHER_EG, cap, -1)


def _gather(pos_rows, aff_t, h2, *, B, S, E, cap):
    D = h2.shape[1]
    ng = E // GATHER_EG
    vmem = 2 * (2 * GATHER_EG * S * 4 + S * D * 2 + GATHER_EG * cap * (D * 2 + LANES * 4)) \
        + GATHER_EG * cap * S * 2 + GATHER_EG * cap * (S + D) * 4
    return pl.pallas_call(
        functools.partial(_gather_kernel, cap=cap),
        grid=(B, ng),
        in_specs=[
            pl.BlockSpec((GATHER_EG, S), lambda b, j: (b * ng + j, 0)),
            pl.BlockSpec((1, GATHER_EG, S), lambda b, j: (b, j, 0)),
            pl.BlockSpec((S, D), lambda b, j: (b, 0)),
        ],
        out_specs=[
            pl.BlockSpec((1, GATHER_EG, cap, D), lambda b, j: (b, j, 0, 0)),
            pl.BlockSpec((1, GATHER_EG, cap, LANES), lambda b, j: (b, j, 0, 0)),
        ],
        out_shape=[
            jax.ShapeDtypeStruct((B, E, cap, D), BF16),
            jax.ShapeDtypeStruct((B, E, cap, LANES), F32),
        ],
        scratch_shapes=[pltpu.VMEM((GATHER_EG * cap, S), BF16)],
        compiler_params=_params(vmem, ("arbitrary", "arbitrary")),
        name="gather",
    )(pos_rows, aff_t, h2)


FFN_FC = 512
FFN_MC = 512


def _ffn_kernel(xs_ref, gate_ref, wg_ref, wu_ref, wd_ref, y_ref, acc_ref):
    f = pl.program_id(1)
    nB, _, cap, D = xs_ref.shape
    per = FFN_MC // cap
    nm = nB // per

    @pl.when(f == 0)
    def _():
        acc_ref[...] = jnp.zeros_like(acc_ref)

    wg = wg_ref[...].astype(BF16)
    wu = wu_ref[...].astype(BF16)
    wd = wd_ref[...].astype(BF16)

    def rows(m):
        return xs_ref[m * per:(m + 1) * per, 0].reshape(FFN_MC, D)

    def gate_up(m):
        x = rows(m)
        return (jnp.dot(x, wg, preferred_element_type=F32),
                jnp.dot(x, wu, preferred_element_type=F32))

    gu = gate_up(0)
    for m in range(nm):
        nxt = gate_up(m + 1) if m + 1 < nm else None
        g, u = gu
        hid = (g * jax.nn.sigmoid(g) * u).astype(BF16)
        acc_ref[m * FFN_MC:(m + 1) * FFN_MC, :] += jnp.dot(hid, wd, preferred_element_type=F32)
        gu = nxt

    @pl.when(f == pl.num_programs(1) - 1)
    def _():
        for m in range(nm):
            gate = gate_ref[m * per:(m + 1) * per, 0].reshape(FFN_MC, LANES)
            y = acc_ref[m * FFN_MC:(m + 1) * FFN_MC, :] * jnp.tile(gate, (1, D // LANES))
            y_ref[m * per:(m + 1) * per, 0] = y.astype(BF16).reshape(per, cap, D)


def _ffn(xs, gates, wg_all, wu_all, wd_all, l):
    B, E, cap, D = xs.shape
    FF = wg_all.shape[3]
    M = B * cap
    vmem = 4 * M * D * 2 + 2 * M * LANES * 4 + 6 * D * FFN_FC * 4 + 3 * D * FFN_FC * 2 + M * D * 4 \
        + 8 * FFN_MC * FFN_FC * 4 + 2 * FFN_MC * D * 4
    return pl.pallas_call(
        _ffn_kernel,
        grid=(E, FF // FFN_FC),
        in_specs=[
            pl.BlockSpec((B, 1, cap, D), lambda e, f: (0, e, 0, 0)),
            pl.BlockSpec((B, 1, cap, LANES), lambda e, f: (0, e, 0, 0)),
            pl.BlockSpec((None, None, D, FFN_FC), lambda e, f: (l, e, 0, f)),
            pl.BlockSpec((None, None, D, FFN_FC), lambda e, f: (l, e, 0, f)),
            pl.BlockSpec((None, None, FFN_FC, D), lambda e, f: (l, e, f, 0)),
        ],
        out_specs=pl.BlockSpec((B, 1, cap, D), lambda e, f: (0, e, 0, 0)),
        out_shape=jax.ShapeDtypeStruct((B, E, cap, D), BF16),
        scratch_shapes=[pltpu.VMEM((M, D), F32)],
        compiler_params=_params(vmem, ("arbitrary", "arbitrary")),
        name="ffn",
    )(xs, gates, wg_all, wu_all, wd_all)


COMBINE_TT = 512


def _combine_kernel(pos_ref, y_ref, x_ref, fg_ref, o_ref, p_ref, *, cap, final):
    E = pos_ref.shape[2]
    slot = lax.broadcasted_iota(jnp.int32, (COMBINE_TT, cap), 1)
    pos = pos_ref[0]
    for e in range(E):
        hit = slot == pos[:, e:e + 1]
        p_ref[:, e * cap:(e + 1) * cap] = jnp.where(hit, 1.0, 0.0).astype(BF16)
    x2 = x_ref[...] + jnp.dot(p_ref[...], y_ref[0], preferred_element_type=F32)
    o_ref[...] = _rms(x2, fg_ref[...]) if final else x2


def _combine(pos_tok, y, x1, final_g, *, B, S, cap, final):
    E = pos_tok.shape[2]
    D = x1.shape[1]
    nt = S // COMBINE_TT
    vmem = 2 * (COMBINE_TT * LANES * 4 + E * cap * D * 2 + 2 * COMBINE_TT * D * 4) \
        + COMBINE_TT * E * cap * 2 + 2 * COMBINE_TT * D * 4 + 2 * COMBINE_TT * cap * 4
    return pl.pallas_call(
        functools.partial(_combine_kernel, cap=cap, final=final),
        grid=(B, nt),
        in_specs=[
            pl.BlockSpec((1, COMBINE_TT, E), lambda b, t: (b, t, 0)),
            pl.BlockSpec((1, E * cap, D), lambda b, t: (b, 0, 0)),
            pl.BlockSpec((COMBINE_TT, D), lambda b, t: (b * nt + t, 0)),
            pl.BlockSpec((1, D), lambda b, t: (0, 0)),
        ],
        out_specs=pl.BlockSpec((COMBINE_TT, D), lambda b, t: (b * nt + t, 0)),
        out_shape=jax.ShapeDtypeStruct((B * S, D), F32),
        scratch_shapes=[pltpu.VMEM((COMBINE_TT, E * cap), BF16)],
        compiler_params=_params(vmem, ("arbitrary", "arbitrary")),
        name="combine",
    )(pos_tok, y.reshape(B, E * cap, D), x1, final_g.reshape(1, D))


def kernel(x, norm1_g, w_in, rpb, conv_w, conv_b, conv_ln_g, conv_ln_b, attn_out_g,
           conv_out_g, w_out, norm2_g, w_router, w_gate, w_up, w_down, final_g):
    B, S, D = x.shape
    depth = w_in.shape[0]
    E = w_router.shape[2]
    cap = CAP_FACTOR * S // E
    stack = lambda a: a.reshape(depth, 1, a.shape[-1])
    w_in_bf = w_in.astype(BF16)
    wr = _router_weights(w_router)
    bias = _attn_bias_tables(rpb, S // GRID_W)
    x2 = x.reshape(B * S, D)
    for l in range(depth):
        qkv, mc = _in_proj_conv(x2, stack(norm1_g), w_in_bf, conv_w, stack(conv_b), stack(conv_ln_g),
                                stack(conv_ln_b), stack(conv_out_g), l, S=S)
        ma = _attn(qkv, bias, stack(attn_out_g), l, B=B, S=S)
        x1, h2, aff = _out_proj(ma, mc, x2, w_out, stack(norm2_g), wr, l, B=B, S=S)
        pos_tok, pos_rows = _route(aff, cap=cap)
        xs, gates = _gather(pos_rows, aff, h2, B=B, S=S, E=E, cap=cap)
        y = _ffn(xs, gates, w_gate, w_up, w_down, l)
        x2 = _combine(pos_tok, y, x1, final_g, B=B, S=S, cap=cap, final=(l == depth - 1))
    return x2.reshape(B, S, D)
```

```python
import functools

import jax
import jax.numpy as jnp
import numpy as np
from jax import lax
from jax.experimental import pallas as pl
from jax.experimental.pallas import tpu as pltpu

F32 = jnp.float32
BF16 = jnp.bfloat16

GRID_W = 64
HEADS = 8
HEAD_DIM = 64
ATTN_W = HEADS * HEAD_DIM
WIN_ROWS = 8
WIN_COLS = 16
CONV_K = 31
N_EXPERTS = 16
CAP_FACTOR = 2
EPS = 1e-6
LOG2E = 1.4426950408889634

LANES = 128
SUBLANES = 8
MIB = 1024 * 1024
VMEM_CAP = 56 * MIB

NEG_BIG = -1e30


def _params(vmem_bytes, sem=None):
    kw = dict(vmem_limit_bytes=int(min(max(vmem_bytes, 16 * MIB), VMEM_CAP)))
    if sem is not None:
        kw["dimension_semantics"] = sem
    return pltpu.CompilerParams(**kw)


def _rms(x, g):
    ms = jnp.mean(x * x, axis=-1, keepdims=True)
    return x * lax.rsqrt(ms + EPS) * g


def _layer_vec(l, n):
    return pl.BlockSpec((None, 1, n), lambda *_: (l, 0, 0))


PROJ_SUB = 512


def _attn_bias_tables(rpb_all, rows):
    kh = min(WIN_ROWS, rows)
    ncol = 2 * WIN_COLS - 1
    nrow = 2 * WIN_ROWS - 1
    qc = np.arange(GRID_W)[:, None]
    kc = np.arange(GRID_W)[None, :]
    col_off = np.clip(kc - qc + (WIN_COLS - 1), 0, ncol - 1)
    col_sel = (col_off[None] == np.arange(ncol)[:, None, None]).astype(np.float32)
    var = np.arange(kh)[:, None]
    kr = np.arange(kh)[None, :]
    row_off = kr - var + (WIN_ROWS - 1)
    row_sel = (row_off[:, :, None] == np.arange(nrow)[None, None, :]).astype(np.float32)
    win_start = np.clip(qc - WIN_COLS // 2, 0, GRID_W - WIN_COLS)
    mask = (kc >= win_start) & (kc < win_start + WIN_COLS)
    L = rpb_all.shape[0]
    rows_picked = jnp.einsum("lhrc,vkr->lvhkc", rpb_all.astype(F32), jnp.asarray(row_sel),
                             precision=lax.Precision.HIGHEST)
    t = jnp.einsum("lvhkc,cqj->lvhqkj", rows_picked, jnp.asarray(col_sel),
                   precision=lax.Precision.HIGHEST)
    t = jnp.where(jnp.asarray(mask)[:, None, :], t * LOG2E, NEG_BIG)
    return t.reshape(L, kh, HEADS // 2, 2 * GRID_W, kh * GRID_W)


def _attn_kernel(qkv_ref, bias_ref, g_ref, o_ref, s0, s1, p0, p1, l0, l1, *, rows, kh):
    nk = kh * GRID_W
    npair = HEADS // 2
    lane = lax.broadcasted_iota(jnp.int32, (GRID_W, LANES), 1)
    first_head = lane < HEAD_DIM
    s_buf, p_buf, l_buf = (s0, s1), (p0, p1), (l0, l1)

    def key_row(r):
        lo, hi = 0, rows - kh
        return min(max(r - kh // 2, lo), hi) if isinstance(r, int) else jnp.clip(r - kh // 2, lo, hi)

    def row_start(r):
        return r * GRID_W if isinstance(r, int) else pl.multiple_of(r * GRID_W, GRID_W)

    def scores(r, slot):
        q0, k0 = row_start(r), row_start(key_row(r))
        for p in range(npair):
            c = p * LANES
            qp = qkv_ref[pl.ds(q0, GRID_W), c:c + LANES].astype(F32)
            qq = jnp.concatenate(
                [jnp.where(first_head, qp, 0.0), jnp.where(first_head, 0.0, qp)], axis=0
            ).astype(BF16)
            kp = qkv_ref[pl.ds(k0, nk), ATTN_W + c:ATTN_W + c + LANES]
            s_buf[slot][p] = lax.dot_general(qq, kp, (((1,), (1,)), ((), ())),
                                             preferred_element_type=F32)

    def softmax(r, slot):
        var = r - key_row(r)
        for p in range(npair):
            s = s_buf[slot][p] + bias_ref[var, p]
            e = jnp.exp2(s - jnp.max(s, axis=-1, keepdims=True))
            l_buf[slot][p] = jnp.broadcast_to(jnp.sum(e, axis=-1, keepdims=True), (2 * GRID_W, LANES))
            p_buf[slot][p] = e.astype(BF16)

    def values(r, slot):
        q0, k0 = row_start(r), row_start(key_row(r))
        outs = []
        for p in range(npair):
            c = 2 * ATTN_W + p * LANES
            vp = qkv_ref[pl.ds(k0, nk), c:c + LANES]
            pv = jnp.dot(p_buf[slot][p], vp, preferred_element_type=F32) / l_buf[slot][p]
            outs.append(jnp.where(first_head, pv[:GRID_W], pv[GRID_W:]))
        o_row = jnp.concatenate(outs, axis=-1)
        o_ref[pl.ds(q0, GRID_W), :] = _rms(o_row, g_ref[...]).astype(BF16)

    scores(0, 0)
    scores(1, 1)
    softmax(0, 0)

    def body(j, carry):
        i = 2 * j
        scores(i, 0)
        softmax(i - 1, 1)
        values(i - 2, 0)
        scores(i + 1, 1)
        softmax(i, 0)
        values(i - 1, 1)
        return carry

    lax.fori_loop(1, rows // 2, body, 0, unroll=3)
    softmax(rows - 1, 1)
    values(rows - 2, 0)
    values(rows - 1, 1)


def _attn(qkv, bias_all, g_all, l, *, B, S):
    rows = S // GRID_W
    kh = min(WIN_ROWS, rows)
    assert rows % 2 == 0 and rows >= 4, "the row pipeline alternates two buffer sets"
    tab = bias_all.shape[1:]
    nk = kh * GRID_W
    npair = HEADS // 2
    ring = 2 * npair * 2 * GRID_W * (nk * 4 + nk * 2 + LANES * 4)
    vmem = 2 * (S * 3 * ATTN_W * 2 + S * ATTN_W * 2 + int(np.prod(tab)) * 4) + ring + 8 * MIB
    buf = lambda n, dt: pltpu.VMEM((npair, 2 * GRID_W, n), dt)
    return pl.pallas_call(
        functools.partial(_attn_kernel, rows=rows, kh=kh),
        grid=(B,),
        in_specs=[
            pl.BlockSpec((S, 3 * ATTN_W), lambda b: (b, 0)),
            pl.BlockSpec((None,) + tab, lambda b: (l, 0, 0, 0, 0)),
            _layer_vec(l, ATTN_W),
        ],
        out_specs=pl.BlockSpec((S, ATTN_W), lambda b: (b, 0)),
        out_shape=jax.ShapeDtypeStruct((B * S, ATTN_W), BF16),
        scratch_shapes=[buf(nk, F32), buf(nk, F32), buf(nk, BF16), buf(nk, BF16),
                        buf(LANES, F32), buf(LANES, F32)],
        compiler_params=_params(vmem, ("arbitrary",)),
        name="attn",
    )(qkv, bias_all, g_all)


CONV_PAD = 16
CONV_TT = 64
def _in_proj_conv_kernel(x_ref, xp_ref, xn_ref, g_ref, w_ref, cw_ref, cb_ref, lg_ref, lb_ref, og_ref,
                         qkv_ref, mc_ref, h_ref, pad_ref, dw_ref, *, C, steps_per_seq):
    tm = x_ref.shape[0]
    step = pl.program_id(0) % steps_per_seq
    g = g_ref[...]
    h_ref[0:CONV_PAD, :] = _rms(xp_ref[...], g).astype(BF16)
    h_ref[CONV_PAD:CONV_PAD + tm, :] = _rms(x_ref[...], g).astype(BF16)
    h_ref[CONV_PAD + tm:, :] = _rms(xn_ref[...], g).astype(BF16)
    ag = jnp.dot(h_ref[...], w_ref[:, 3 * ATTN_W:], preferred_element_type=F32)
    u = ag[:, :C] * jax.nn.sigmoid(ag[:, C:])
    row = lax.broadcasted_iota(jnp.int32, (tm + 2 * CONV_PAD, 1), 0)
    first_in = jnp.where(step == 0, CONV_PAD, 0)
    end_in = jnp.where(step == steps_per_seq - 1, tm + CONV_PAD, tm + 2 * CONV_PAD)
    u = jnp.where((row >= first_in) & (row < end_in), u, 0.0)
    for c in range(C // LANES):
        pad_ref[c] = u[:, c * LANES:(c + 1) * LANES]

    h_main = h_ref[CONV_PAD:CONV_PAD + tm, :]
    anchors = {}
    n_ln = tm // CONV_TT
    for i, n0 in enumerate(range(0, 3 * ATTN_W, ATTN_W)):
        r = jnp.dot(h_main, w_ref[:, n0:n0 + ATTN_W], preferred_element_type=F32)
        if n0 == 0:
            r = r * (HEAD_DIM ** -0.5 * LOG2E)
        qkv_ref[:, n0:n0 + ATTN_W] = r.astype(BF16)
        bits = pltpu.bitcast(r[0:SUBLANES, 0:LANES], jnp.uint32)
        sixteen = jnp.uint32(16)
        zero = pltpu.bitcast(lax.shift_right_logical(lax.shift_right_logical(bits, sixteen), sixteen), F32)
        anchors[(i + 1) * n_ln // 4 - 1] = zero

    shift = CONV_PAD - CONV_K // 2
    nsub = CONV_TT // SUBLANES
    for c in range(C // LANES):
        sl = slice(c * LANES, (c + 1) * LANES)
        wb = [jnp.broadcast_to(cw_ref[k:k + 1, sl], (SUBLANES, LANES)) for k in range(CONV_K)]
        for t0 in range(0, tm, CONV_TT):
            accs = [jnp.zeros((SUBLANES, LANES), F32) for _ in range(nsub)]
            for off in range(SUBLANES * (nsub - 1) + CONV_K):
                r0 = t0 + shift + off
                win = pad_ref[c, r0:r0 + SUBLANES, :]
                for j in range(nsub):
                    k = off - SUBLANES * j
                    if 0 <= k < CONV_K:
                        accs[j] = accs[j] + win * wb[k]
            dw_ref[t0:t0 + CONV_TT, sl] = jnp.concatenate(accs, axis=0)
    for t0 in range(0, tm, CONV_TT):
        dw = dw_ref[t0:t0 + CONV_TT, :] + cb_ref[...]
        if t0 // CONV_TT in anchors:
            dw = dw + jnp.tile(anchors[t0 // CONV_TT], (CONV_TT // SUBLANES, C // LANES))
        mu = jnp.mean(dw, axis=-1, keepdims=True)
        d = dw - mu
        var = jnp.mean(d * d, axis=-1, keepdims=True)
        y = d * lax.rsqrt(var + EPS) * lg_ref[...] + lb_ref[...]
        z = y * jax.nn.sigmoid(y)
        mc_ref[t0:t0 + CONV_TT, :] = _rms(z, og_ref[...]).astype(BF16)


def _in_proj_conv(x2, g_all, w_all, cw_all, cb_all, lg_all, lb_all, og_all, l, *, S, tm=2 * PROJ_SUB):
    T, D = x2.shape
    N = w_all.shape[2]
    C = (N - 3 * ATTN_W) // 2
    nt = S // tm
    hb = tm // CONV_PAD
    last = T // CONV_PAD - 1
    vmem = 2 * (tm * D * 4 + tm * 3 * ATTN_W * 2 + tm * C * 2) + 2 * D * N * 2 \
        + (tm + 2 * CONV_PAD) * (D * 2 + 2 * C * 4 + 2 * C * 4) + tm * 3 * ATTN_W * 4 + tm * C * 4 + 4 * MIB
    vec = _layer_vec(l, C)
    return pl.pallas_call(
        functools.partial(_in_proj_conv_kernel, C=C, steps_per_seq=nt),
        grid=(T // tm,),
        in_specs=[
            pl.BlockSpec((tm, D), lambda i: (i, 0)),
            pl.BlockSpec((CONV_PAD, D), lambda i: (jnp.maximum(i * hb - 1, 0), 0)),
            pl.BlockSpec((CONV_PAD, D), lambda i: (jnp.minimum((i + 1) * hb, last), 0)),
            _layer_vec(l, D),
            pl.BlockSpec((None, D, N), lambda i: (l, 0, 0)),
            pl.BlockSpec((None, CONV_K, C), lambda i: (l, 0, 0)),
            vec, vec, vec, vec,
        ],
        out_specs=[
            pl.BlockSpec((tm, 3 * ATTN_W), lambda i: (i, 0)),
            pl.BlockSpec((tm, C), lambda i: (i, 0)),
        ],
        out_shape=[
            jax.ShapeDtypeStruct((T, 3 * ATTN_W), BF16),
            jax.ShapeDtypeStruct((T, C), BF16),
        ],
        scratch_shapes=[pltpu.VMEM((tm + 2 * CONV_PAD, D), BF16),
                        pltpu.VMEM((C // LANES, tm + 2 * CONV_PAD, LANES), F32),
                        pltpu.VMEM((tm, C), F32)],
        compiler_params=_params(vmem, ("arbitrary",)),
        name="in_proj_conv",
    )(x2, x2, x2, g_all, w_all, cw_all, cb_all, lg_all, lb_all, og_all)


def _router_weights(w_router_all):
    wt = jnp.swapaxes(w_router_all, 1, 2)
    hi = wt.astype(BF16)
    lo = (wt - hi.astype(F32)).astype(BF16)
    return jnp.concatenate([hi, lo], axis=1)


def _router_affinity_t(h_bf16, wr_t):
    lt = lax.dot_general(wr_t, h_bf16, (((1,), (1,)), ((), ())), preferred_element_type=F32)
    logits = lt[:N_EXPERTS] + lt[N_EXPERTS:]
    e = jnp.exp(logits - jnp.max(logits, axis=0, keepdims=True))
    return e / jnp.sum(e, axis=0, keepdims=True)


def _out_proj_kernel(ma_ref, mc_ref, x_ref, w_ref, g_ref, wr_ref, x1_ref, h2_ref, aff_ref):
    wa = ma_ref.shape[1]
    subs = [slice(i, i + PROJ_SUB) for i in range(0, x_ref.shape[0], PROJ_SUB)]
    w_attn = w_ref[:wa].astype(BF16)
    w_conv = w_ref[wa:].astype(BF16)
    accs = [jnp.dot(ma_ref[r, :], w_attn, preferred_element_type=F32)
            + jnp.dot(mc_ref[r, :], w_conv, preferred_element_type=F32) for r in subs]
    for r, acc in zip(subs, accs):
        x1 = x_ref[r, :] + acc
        x1_ref[r, :] = x1
        h = _rms(x1, g_ref[...]).astype(BF16)
        h2_ref[r, :] = h
        aff_ref[:, r] = _router_affinity_t(h, wr_ref[...])


def _out_proj(ma, mc, x2, w_all, g_all, wr_all, l, *, B, S, tm=2 * PROJ_SUB):
    T, D = x2.shape
    Wa = ma.shape[1]
    Wc = mc.shape[1]
    nt = S // tm
    vmem = 2 * (tm * (Wa + Wc) * 2 + 2 * tm * D * 4 + tm * D * 2 + N_EXPERTS * tm * 4) \
        + 2 * (D * D * 4 + 2 * N_EXPERTS * D * 2) + D * D * 2 + 2 * tm * D * 4 + tm * D * 2 + 4 * MIB
    return pl.pallas_call(
        _out_proj_kernel,
        grid=(T // tm,),
        in_specs=[
            pl.BlockSpec((tm, Wa), lambda i: (i, 0)),
            pl.BlockSpec((tm, Wc), lambda i: (i, 0)),
            pl.BlockSpec((tm, D), lambda i: (i, 0)),
            pl.BlockSpec((None, Wa + Wc, D), lambda i: (l, 0, 0)),
            _layer_vec(l, D),
            pl.BlockSpec((None, 2 * N_EXPERTS, D), lambda i: (l, 0, 0)),
        ],
        out_specs=[
            pl.BlockSpec((tm, D), lambda i: (i, 0)),
            pl.BlockSpec((tm, D), lambda i: (i, 0)),
            pl.BlockSpec((None, N_EXPERTS, tm), lambda i: (i // nt, 0, i % nt)),
        ],
        out_shape=[
            jax.ShapeDtypeStruct((T, D), F32),
            jax.ShapeDtypeStruct((T, D), BF16),
            jax.ShapeDtypeStruct((B, N_EXPERTS, S), F32),
        ],
        compiler_params=_params(vmem, ("arbitrary",)),
        name="out_proj",
    )(ma, mc, x2, w_all, g_all, wr_all)


ROUTE_BLK = 256


def _excl_cumsum_rows(mask_f32, tri):
    S = mask_f32.shape[0]
    carry = jnp.zeros((1, mask_f32.shape[1]), F32)
    out = []
    for i in range(S // ROUTE_BLK):
        blk = mask_f32[i * ROUTE_BLK:(i + 1) * ROUTE_BLK]
        out.append(jnp.dot(tri, blk.astype(BF16), preferred_element_type=F32) + carry)
        carry = carry + jnp.sum(blk, axis=0, keepdims=True)
    return jnp.concatenate(out, axis=0)


def _route_kernel(aff_ref, pos_tok_ref, pos_rows_ref, *, cap):
    B, E, S = aff_ref.shape
    N = B * E
    aff = aff_ref[...].reshape(N, S).T
    bits = pltpu.bitcast(aff, jnp.int32)

    def bisect(_, lh):
        lo, hi = lh
        mid = lo + ((hi - lo + 1) >> 1)
        cnt = jnp.sum((bits >= mid).astype(jnp.int32), axis=0, keepdims=True)
        ok = cnt >= cap
        return jnp.where(ok, mid, lo), jnp.where(ok, hi, mid - 1)

    lo0 = jnp.zeros((1, N), jnp.int32)
    hi0 = jnp.full((1, N), 0x7F800000, jnp.int32)
    thr, _ = lax.fori_loop(0, 31, bisect, (lo0, hi0))

    r = lax.broadcasted_iota(jnp.int32, (ROUTE_BLK, ROUTE_BLK), 0)
    c = lax.broadcasted_iota(jnp.int32, (ROUTE_BLK, ROUTE_BLK), 1)
    tri = jnp.where(r > c, 1.0, 0.0).astype(BF16)

    gt = jnp.where(bits > thr, 1.0, 0.0)
    eq = jnp.where(bits == thr, 1.0, 0.0)
    need = cap - jnp.sum(gt, axis=0, keepdims=True)
    sel = gt + eq * jnp.where(_excl_cumsum_rows(eq, tri) < need, 1.0, 0.0)
    pos = jnp.where(sel > 0.5, _excl_cumsum_rows(sel, tri), -1.0)
    for b in range(B):
        pos_tok_ref[b] = pos[:, b * E:(b + 1) * E].astype(jnp.int32)
    pos_rows_ref[...] = pos.T.astype(jnp.int32)


def _route(aff_t, *, cap):
    B, E, S = aff_t.shape
    return pl.pallas_call(
        functools.partial(_route_kernel, cap=cap),
        out_shape=[jax.ShapeDtypeStruct((B, S, E), jnp.int32),
                   jax.ShapeDtypeStruct((B * E, S), jnp.int32)],
        compiler_params=_params(32 * MIB),
        name="route",
    )(aff_t)


GATHER_EG = 8


def _gather_kernel(pos_ref, aff_ref, h_ref, xs_ref, gate_ref, p_ref, *, cap):
    S = h_ref.shape[0]
    slot = lax.broadcasted_iota(jnp.int32, (cap, S), 0)
    for e in range(GATHER_EG):
        hit = slot == pos_ref[e:e + 1, :]
        p_ref[e * cap:(e + 1) * cap, :] = jnp.where(hit, 1.0, 0.0).astype(BF16)
        g = jnp.sum(jnp.where(hit, aff_ref[0, e:e + 1, :], 0.0), axis=-1, keepdims=True)
        gate_ref[0, e] = jnp.broadcast_to(g, (cap, LANES))
    xs = jnp.dot(p_ref[...], h_ref[...], preferred_element_type=F32)
    xs_ref[0] = xs.astype(BF16).reshape(GATHER_EG, cap, -1)


def _gather(pos_rows, aff_t, h2, *, B, S, E, cap):
    D = h2.shape[1]
    ng = E // GATHER_EG
    vmem = 2 * (2 * GATHER_EG * S * 4 + S * D * 2 + GATHER_EG * cap * (D * 2 + LANES * 4)) \
        + GATHER_EG * cap * S * 2 + GATHER_EG * cap * (S + D) * 4
    return pl.pallas_call(
        functools.partial(_gather_kernel, cap=cap),
        grid=(B, ng),
        in_specs=[
            pl.BlockSpec((GATHER_EG, S), lambda b, j: (b * ng + j, 0)),
            pl.BlockSpec((1, GATHER_EG, S), lambda b, j: (b, j, 0)),
            pl.BlockSpec((S, D), lambda b, j: (b, 0)),
        ],
        out_specs=[
            pl.BlockSpec((1, GATHER_EG, cap, D), lambda b, j: (b, j, 0, 0)),
            pl.BlockSpec((1, GATHER_EG, cap, LANES), lambda b, j: (b, j, 0, 0)),
        ],
        out_shape=[
            jax.ShapeDtypeStruct((B, E, cap, D), BF16),
            jax.ShapeDtypeStruct((B, E, cap, LANES), F32),
        ],
        scratch_shapes=[pltpu.VMEM((GATHER_EG * cap, S), BF16)],
        compiler_params=_params(vmem, ("arbitrary", "arbitrary")),
        name="gather",
    )(pos_rows, aff_t, h2)


FFN_FC = 512
FFN_MC = 512


def _ffn_kernel(xs_ref, gate_ref, wg_ref, wu_ref, wd_ref, y_ref, acc_ref):
    f = pl.program_id(1)
    nB, _, cap, D = xs_ref.shape
    per = FFN_MC // cap
    nm = nB // per

    @pl.when(f == 0)
    def _():
        acc_ref[...] = jnp.zeros_like(acc_ref)

    wg = wg_ref[...].astype(BF16)
    wu = wu_ref[...].astype(BF16)
    wd = wd_ref[...].astype(BF16)

    def rows(m):
        return xs_ref[m * per:(m + 1) * per, 0].reshape(FFN_MC, D)

    def gate_up(m):
        x = rows(m)
        return (jnp.dot(x, wg, preferred_element_type=F32),
                jnp.dot(x, wu, preferred_element_type=F32))

    gu = gate_up(0)
    for m in range(nm):
        nxt = gate_up(m + 1) if m + 1 < nm else None
        g, u = gu
        hid = (g * jax.nn.sigmoid(g) * u).astype(BF16)
        acc_ref[m * FFN_MC:(m + 1) * FFN_MC, :] += jnp.dot(hid, wd, preferred_element_type=F32)
        gu = nxt

    @pl.when(f == pl.num_programs(1) - 1)
    def _():
        for m in range(nm):
            gate = gate_ref[m * per:(m + 1) * per, 0].reshape(FFN_MC, LANES)
            y = acc_ref[m * FFN_MC:(m + 1) * FFN_MC, :] * jnp.tile(gate, (1, D // LANES))
            y_ref[m * per:(m + 1) * per, 0] = y.astype(BF16).reshape(per, cap, D)


def _ffn(xs, gates, wg_all, wu_all, wd_all, l):
    B, E, cap, D = xs.shape
    FF = wg_all.shape[3]
    M = B * cap
    vmem = 4 * M * D * 2 + 2 * M * LANES * 4 + 6 * D * FFN_FC * 4 + 3 * D * FFN_FC * 2 + M * D * 4 \
        + 8 * FFN_MC * FFN_FC * 4 + 2 * FFN_MC * D * 4
    return pl.pallas_call(
        _ffn_kernel,
        grid=(E, FF // FFN_FC),
        in_specs=[
            pl.BlockSpec((B, 1, cap, D), lambda e, f: (0, e, 0, 0)),
            pl.BlockSpec((B, 1, cap, LANES), lambda e, f: (0, e, 0, 0)),
            pl.BlockSpec((None, None, D, FFN_FC), lambda e, f: (l, e, 0, f)),
            pl.BlockSpec((None, None, D, FFN_FC), lambda e, f: (l, e, 0, f)),
            pl.BlockSpec((None, None, FFN_FC, D), lambda e, f: (l, e, f, 0)),
        ],
        out_specs=pl.BlockSpec((B, 1, cap, D), lambda e, f: (0, e, 0, 0)),
        out_shape=jax.ShapeDtypeStruct((B, E, cap, D), BF16),
        scratch_shapes=[pltpu.VMEM((M, D), F32)],
        compiler_params=_params(vmem, ("arbitrary", "arbitrary")),
        name="ffn",
    )(xs, gates, wg_all, wu_all, wd_all)


COMBINE_TT = 512


def _combine_kernel(pos_ref, y_ref, x_ref, fg_ref, o_ref, p_ref, *, cap, final):
    E = pos_ref.shape[2]
    slot = lax.broadcasted_iota(jnp.int32, (COMBINE_TT, cap), 1)
    pos = pos_ref[0]
    for e in range(E):
        hit = slot == pos[:, e:e + 1]
        p_ref[:, e * cap:(e + 1) * cap] = jnp.where(hit, 1.0, 0.0).astype(BF16)
    x2 = x_ref[...] + jnp.dot(p_ref[...], y_ref[0], preferred_element_type=F32)
    o_ref[...] = _rms(x2, fg_ref[...]) if final else x2


def _combine(pos_tok, y, x1, final_g, *, B, S, cap, final):
    E = pos_tok.shape[2]
    D = x1.shape[1]
    nt = S // COMBINE_TT
    vmem = 2 * (COMBINE_TT * LANES * 4 + E * cap * D * 2 + 2 * COMBINE_TT * D * 4) \
        + COMBINE_TT * E * cap * 2 + 2 * COMBINE_TT * D * 4 + 2 * COMBINE_TT * cap * 4
    return pl.pallas_call(
        functools.partial(_combine_kernel, cap=cap, final=final),
        grid=(B, nt),
        in_specs=[
            pl.BlockSpec((1, COMBINE_TT, E), lambda b, t: (b, t, 0)),
            pl.BlockSpec((1, E * cap, D), lambda b, t: (b, 0, 0)),
            pl.BlockSpec((COMBINE_TT, D), lambda b, t: (b * nt + t, 0)),
            pl.BlockSpec((1, D), lambda b, t: (0, 0)),
        ],
        out_specs=pl.BlockSpec((COMBINE_TT, D), lambda b, t: (b * nt + t, 0)),
        out_shape=jax.ShapeDtypeStruct((B * S, D), F32),
        scratch_shapes=[pltpu.VMEM((COMBINE_TT, E * cap), BF16)],
        compiler_params=_params(vmem, ("arbitrary", "arbitrary")),
        name="combine",
    )(pos_tok, y.reshape(B, E * cap, D), x1, final_g.reshape(1, D))


def kernel(x, norm1_g, w_in, rpb, conv_w, conv_b, conv_ln_g, conv_ln_b, attn_out_g,
           conv_out_g, w_out, norm2_g, w_router, w_gate, w_up, w_down, final_g):
    B, S, D = x.shape
    depth = w_in.shape[0]
    E = w_router.shape[2]
    cap = CAP_FACTOR * S // E
    stack = lambda a: a.reshape(depth, 1, a.shape[-1])
    w_in_bf = w_in.astype(BF16)
    wr = _router_weights(w_router)
    bias = _attn_bias_tables(rpb, S // GRID_W)
    x2 = x.reshape(B * S, D)
    for l in range(depth):
        qkv, mc = _in_proj_conv(x2, stack(norm1_g), w_in_bf, conv_w, stack(conv_b), stack(conv_ln_g),
                                stack(conv_ln_b), stack(conv_out_g), l, S=S)
        ma = _attn(qkv, bias, stack(attn_out_g), l, B=B, S=S)
        x1, h2, aff = _out_proj(ma, mc, x2, w_out, stack(norm2_g), wr, l, B=B, S=S)
        pos_tok, pos_rows = _route(aff, cap=cap)
        xs, gates = _gather(pos_rows, aff, h2, B=B, S=S, E=E, cap=cap)
        y = _ffn(xs, gates, w_gate, w_up, w_down, l)
        x2 = _combine(pos_tok, y, x1, final_g, B=B, S=S, cap=cap, final=(l == depth - 1))
    return x2.reshape(B, S, D)
```

```python
import functools

import jax
import jax.numpy as jnp
import numpy as np
from jax import lax
from jax.experimental import pallas as pl
from jax.experimental.pallas import tpu as pltpu

F32 = jnp.float32
BF16 = jnp.bfloat16

GRID_W = 64
HEADS = 8
HEAD_DIM = 64
ATTN_W = HEADS * HEAD_DIM
WIN_ROWS = 8
WIN_COLS = 16
CONV_K = 31
N_EXPERTS = 16
CAP_FACTOR = 2
EPS = 1e-6
LOG2E = 1.4426950408889634

LANES = 128
SUBLANES = 8
MIB = 1024 * 1024
VMEM_CAP = 56 * MIB

NEG_BIG = -1e30


def _params(vmem_bytes, sem=None):
    kw = dict(vmem_limit_bytes=int(min(max(vmem_bytes, 16 * MIB), VMEM_CAP)))
    if sem is not None:
        kw["dimension_semantics"] = sem
    return pltpu.CompilerParams(**kw)


def _rms(x, g):
    ms = jnp.mean(x * x, axis=-1, keepdims=True)
    return x * lax.rsqrt(ms + EPS) * g


def _layer_vec(l, n):
    return pl.BlockSpec((None, 1, n), lambda *_: (l, 0, 0))


PROJ_SUB = 512


def _attn_bias_tables(rpb_all, rows):
    kh = min(WIN_ROWS, rows)
    ncol = 2 * WIN_COLS - 1
    nrow = 2 * WIN_ROWS - 1
    qc = np.arange(GRID_W)[:, None]
    kc = np.arange(GRID_W)[None, :]
    col_off = np.clip(kc - qc + (WIN_COLS - 1), 0, ncol - 1)
    col_sel = (col_off[None] == np.arange(ncol)[:, None, None]).astype(np.float32)
    var = np.arange(kh)[:, None]
    kr = np.arange(kh)[None, :]
    row_off = kr - var + (WIN_ROWS - 1)
    row_sel = (row_off[:, :, None] == np.arange(nrow)[None, None, :]).astype(np.float32)
    win_start = np.clip(qc - WIN_COLS // 2, 0, GRID_W - WIN_COLS)
    mask = (kc >= win_start) & (kc < win_start + WIN_COLS)
    L = rpb_all.shape[0]
    rows_picked = jnp.einsum("lhrc,vkr->lvhkc", rpb_all.astype(F32), jnp.asarray(row_sel),
                             precision=lax.Precision.HIGHEST)
    t = jnp.einsum("lvhkc,cqj->lvhqkj", rows_picked, jnp.asarray(col_sel),
                   precision=lax.Precision.HIGHEST)
    t = jnp.where(jnp.asarray(mask)[:, None, :], t * LOG2E, NEG_BIG)
    return t.reshape(L, kh, HEADS // 2, 2 * GRID_W, kh * GRID_W)


def _attn_kernel(qkv_ref, bias_ref, g_ref, o_ref, s0, s1, p0, p1, l0, l1, *, rows, kh):
    nk = kh * GRID_W
    npair = HEADS // 2
    lane = lax.broadcasted_iota(jnp.int32, (GRID_W, LANES), 1)
    first_head = lane < HEAD_DIM
    s_buf, p_buf, l_buf = (s0, s1), (p0, p1), (l0, l1)

    def key_row(r):
        lo, hi = 0, rows - kh
        return min(max(r - kh // 2, lo), hi) if isinstance(r, int) else jnp.clip(r - kh // 2, lo, hi)

    def row_start(r):
        return r * GRID_W if isinstance(r, int) else pl.multiple_of(r * GRID_W, GRID_W)

    def scores(r, slot):
        q0, k0 = row_start(r), row_start(key_row(r))
        for p in range(npair):
            c = p * LANES
            qp = qkv_ref[pl.ds(q0, GRID_W), c:c + LANES].astype(F32)
            qq = jnp.concatenate(
                [jnp.where(first_head, qp, 0.0), jnp.where(first_head, 0.0, qp)], axis=0
            ).astype(BF16)
            kp = qkv_ref[pl.ds(k0, nk), ATTN_W + c:ATTN_W + c + LANES]
            s_buf[slot][p] = lax.dot_general(qq, kp, (((1,), (1,)), ((), ())),
                                             preferred_element_type=F32)

    def softmax(r, slot):
        var = r - key_row(r)
        for p in range(npair):
            s = s_buf[slot][p] + bias_ref[var, p]
            e = jnp.exp2(s - jnp.max(s, axis=-1, keepdims=True))
            l_buf[slot][p] = jnp.broadcast_to(jnp.sum(e, axis=-1, keepdims=True), (2 * GRID_W, LANES))
            p_buf[slot][p] = e.astype(BF16)

    def values(r, slot):
        q0, k0 = row_start(r), row_start(key_row(r))
        outs = []
        for p in range(npair):
            c = 2 * ATTN_W + p * LANES
            vp = qkv_ref[pl.ds(k0, nk), c:c + LANES]
            pv = jnp.dot(p_buf[slot][p], vp, preferred_element_type=F32) / l_buf[slot][p]
            outs.append(jnp.where(first_head, pv[:GRID_W], pv[GRID_W:]))
        o_row = jnp.concatenate(outs, axis=-1)
        o_ref[pl.ds(q0, GRID_W), :] = _rms(o_row, g_ref[...]).astype(BF16)

    scores(0, 0)
    scores(1, 1)
    softmax(0, 0)

    def body(j, carry):
        i = 2 * j
        scores(i, 0)
        softmax(i - 1, 1)
        values(i - 2, 0)
        scores(i + 1, 1)
        softmax(i, 0)
        values(i - 1, 1)
        return carry

    lax.fori_loop(1, rows // 2, body, 0, unroll=5)
    softmax(rows - 1, 1)
    values(rows - 2, 0)
    values(rows - 1, 1)


def _attn(qkv, bias_all, g_all, l, *, B, S):
    rows = S // GRID_W
    kh = min(WIN_ROWS, rows)
    assert rows % 2 == 0 and rows >= 4, "the row pipeline alternates two buffer sets"
    tab = bias_all.shape[1:]
    nk = kh * GRID_W
    npair = HEADS // 2
    ring = 2 * npair * 2 * GRID_W * (nk * 4 + nk * 2 + LANES * 4)
    vmem = 2 * (S * 3 * ATTN_W * 2 + S * ATTN_W * 2 + int(np.prod(tab)) * 4) + ring + 8 * MIB
    buf = lambda n, dt: pltpu.VMEM((npair, 2 * GRID_W, n), dt)
    return pl.pallas_call(
        functools.partial(_attn_kernel, rows=rows, kh=kh),
        grid=(B,),
        in_specs=[
            pl.BlockSpec((S, 3 * ATTN_W), lambda b: (b, 0)),
            pl.BlockSpec((None,) + tab, lambda b: (l, 0, 0, 0, 0)),
            _layer_vec(l, ATTN_W),
        ],
        out_specs=pl.BlockSpec((S, ATTN_W), lambda b: (b, 0)),
        out_shape=jax.ShapeDtypeStruct((B * S, ATTN_W), BF16),
        scratch_shapes=[buf(nk, F32), buf(nk, F32), buf(nk, BF16), buf(nk, BF16),
                        buf(LANES, F32), buf(LANES, F32)],
        compiler_params=_params(vmem, ("arbitrary",)),
        name="attn",
    )(qkv, bias_all, g_all)


CONV_PAD = 16
CONV_TT = 64
def _in_proj_conv_kernel(x_ref, xp_ref, xn_ref, g_ref, w_ref, cw_ref, cb_ref, lg_ref, lb_ref, og_ref,
                         qkv_ref, mc_ref, h_ref, pad_ref, dw_ref, *, C, steps_per_seq):
    tm = x_ref.shape[0]
    step = pl.program_id(0) % steps_per_seq
    g = g_ref[...]
    h_ref[0:CONV_PAD, :] = _rms(xp_ref[...], g).astype(BF16)
    h_ref[CONV_PAD:CONV_PAD + tm, :] = _rms(x_ref[...], g).astype(BF16)
    h_ref[CONV_PAD + tm:, :] = _rms(xn_ref[...], g).astype(BF16)
    ag = jnp.dot(h_ref[...], w_ref[:, 3 * ATTN_W:], preferred_element_type=F32)
    u = ag[:, :C] * jax.nn.sigmoid(ag[:, C:])
    row = lax.broadcasted_iota(jnp.int32, (tm + 2 * CONV_PAD, 1), 0)
    first_in = jnp.where(step == 0, CONV_PAD, 0)
    end_in = jnp.where(step == steps_per_seq - 1, tm + CONV_PAD, tm + 2 * CONV_PAD)
    u = jnp.where((row >= first_in) & (row < end_in), u, 0.0)
    for c in range(C // LANES):
        pad_ref[c] = u[:, c * LANES:(c + 1) * LANES]

    h_main = h_ref[CONV_PAD:CONV_PAD + tm, :]
    anchors = {}
    n_ln = tm // CONV_TT
    for i, n0 in enumerate(range(0, 3 * ATTN_W, ATTN_W)):
        r = jnp.dot(h_main, w_ref[:, n0:n0 + ATTN_W], preferred_element_type=F32)
        if n0 == 0:
            r = r * (HEAD_DIM ** -0.5 * LOG2E)
        qkv_ref[:, n0:n0 + ATTN_W] = r.astype(BF16)
        bits = pltpu.bitcast(r[0:SUBLANES, 0:LANES], jnp.uint32)
        sixteen = jnp.uint32(16)
        zero = pltpu.bitcast(lax.shift_right_logical(lax.shift_right_logical(bits, sixteen), sixteen), F32)
        anchors[(i + 1) * n_ln // 4 - 1] = zero

    shift = CONV_PAD - CONV_K // 2
    nsub = CONV_TT // SUBLANES
    for c in range(C // LANES):
        sl = slice(c * LANES, (c + 1) * LANES)
        wb = [jnp.broadcast_to(cw_ref[k:k + 1, sl], (SUBLANES, LANES)) for k in range(CONV_K)]
        for t0 in range(0, tm, CONV_TT):
            accs = [jnp.zeros((SUBLANES, LANES), F32) for _ in range(nsub)]
            for off in range(SUBLANES * (nsub - 1) + CONV_K):
                r0 = t0 + shift + off
                win = pad_ref[c, r0:r0 + SUBLANES, :]
                for j in range(nsub):
                    k = off - SUBLANES * j
                    if 0 <= k < CONV_K:
                        accs[j] = accs[j] + win * wb[k]
            dw_ref[t0:t0 + CONV_TT, sl] = jnp.concatenate(accs, axis=0)
    for t0 in range(0, tm, CONV_TT):
        dw = dw_ref[t0:t0 + CONV_TT, :] + cb_ref[...]
        if t0 // CONV_TT in anchors:
            dw = dw + jnp.tile(anchors[t0 // CONV_TT], (CONV_TT // SUBLANES, C // LANES))
        mu = jnp.mean(dw, axis=-1, keepdims=True)
        d = dw - mu
        var = jnp.mean(d * d, axis=-1, keepdims=True)
        y = d * lax.rsqrt(var + EPS) * lg_ref[...] + lb_ref[...]
        z = y * jax.nn.sigmoid(y)
        mc_ref[t0:t0 + CONV_TT, :] = _rms(z, og_ref[...]).astype(BF16)


def _in_proj_conv(x2, g_all, w_all, cw_all, cb_all, lg_all, lb_all, og_all, l, *, S, tm=2 * PROJ_SUB):
    T, D = x2.shape
    N = w_all.shape[2]
    C = (N - 3 * ATTN_W) // 2
    nt = S // tm
    hb = tm // CONV_PAD
    last = T // CONV_PAD - 1
    vmem = 2 * (tm * D * 4 + tm * 3 * ATTN_W * 2 + tm * C * 2) + 2 * D * N * 2 \
        + (tm + 2 * CONV_PAD) * (D * 2 + 2 * C * 4 + 2 * C * 4) + tm * 3 * ATTN_W * 4 + tm * C * 4 + 4 * MIB
    vec = _layer_vec(l, C)
    return pl.pallas_call(
        functools.partial(_in_proj_conv_kernel, C=C, steps_per_seq=nt),
        grid=(T // tm,),
        in_specs=[
            pl.BlockSpec((tm, D), lambda i: (i, 0)),
            pl.BlockSpec((CONV_PAD, D), lambda i: (jnp.maximum(i * hb - 1, 0), 0)),
            pl.BlockSpec((CONV_PAD, D), lambda i: (jnp.minimum((i + 1) * hb, last), 0)),
            _layer_vec(l, D),
            pl.BlockSpec((None, D, N), lambda i: (l, 0, 0)),
            pl.BlockSpec((None, CONV_K, C), lambda i: (l, 0, 0)),
            vec, vec, vec, vec,
        ],
        out_specs=[
            pl.BlockSpec((tm, 3 * ATTN_W), lambda i: (i, 0)),
            pl.BlockSpec((tm, C), lambda i: (i, 0)),
        ],
        out_shape=[
            jax.ShapeDtypeStruct((T, 3 * ATTN_W), BF16),
            jax.ShapeDtypeStruct((T, C), BF16),
        ],
        scratch_shapes=[pltpu.VMEM((tm + 2 * CONV_PAD, D), BF16),
                        pltpu.VMEM((C // LANES, tm + 2 * CONV_PAD, LANES), F32),
                        pltpu.VMEM((tm, C), F32)],
        compiler_params=_params(vmem, ("arbitrary",)),
        name="in_proj_conv",
    )(x2, x2, x2, g_all, w_all, cw_all, cb_all, lg_all, lb_all, og_all)


def _router_weights(w_router_all):
    wt = jnp.swapaxes(w_router_all, 1, 2)
    hi = wt.astype(BF16)
    lo = (wt - hi.astype(F32)).astype(BF16)
    return jnp.concatenate([hi, lo], axis=1)


def _router_affinity_t(h_bf16, wr_t):
    lt = lax.dot_general(wr_t, h_bf16, (((1,), (1,)), ((), ())), preferred_element_type=F32)
    logits = lt[:N_EXPERTS] + lt[N_EXPERTS:]
    e = jnp.exp(logits - jnp.max(logits, axis=0, keepdims=True))
    return e / jnp.sum(e, axis=0, keepdims=True)


def _out_proj_kernel(ma_ref, mc_ref, x_ref, w_ref, g_ref, wr_ref, x1_ref, h2_ref, aff_ref):
    wa = ma_ref.shape[1]
    subs = [slice(i, i + PROJ_SUB) for i in range(0, x_ref.shape[0], PROJ_SUB)]
    w_attn = w_ref[:wa].astype(BF16)
    w_conv = w_ref[wa:].astype(BF16)
    accs = [jnp.dot(ma_ref[r, :], w_attn, preferred_element_type=F32)
            + jnp.dot(mc_ref[r, :], w_conv, preferred_element_type=F32) for r in subs]
    for r, acc in zip(subs, accs):
        x1 = x_ref[r, :] + acc
        x1_ref[r, :] = x1
        h = _rms(x1, g_ref[...]).astype(BF16)
        h2_ref[r, :] = h
        aff_ref[:, r] = _router_affinity_t(h, wr_ref[...])


def _out_proj(ma, mc, x2, w_all, g_all, wr_all, l, *, B, S, tm=2 * PROJ_SUB):
    T, D = x2.shape
    Wa = ma.shape[1]
    Wc = mc.shape[1]
    nt = S // tm
    vmem = 2 * (tm * (Wa + Wc) * 2 + 2 * tm * D * 4 + tm * D * 2 + N_EXPERTS * tm * 4) \
        + 2 * (D * D * 4 + 2 * N_EXPERTS * D * 2) + D * D * 2 + 2 * tm * D * 4 + tm * D * 2 + 4 * MIB
    return pl.pallas_call(
        _out_proj_kernel,
        grid=(T // tm,),
        in_specs=[
            pl.BlockSpec((tm, Wa), lambda i: (i, 0)),
            pl.BlockSpec((tm, Wc), lambda i: (i, 0)),
            pl.BlockSpec((tm, D), lambda i: (i, 0)),
            pl.BlockSpec((None, Wa + Wc, D), lambda i: (l, 0, 0)),
            _layer_vec(l, D),
            pl.BlockSpec((None, 2 * N_EXPERTS, D), lambda i: (l, 0, 0)),
        ],
        out_specs=[
            pl.BlockSpec((tm, D), lambda i: (i, 0)),
            pl.BlockSpec((tm, D), lambda i: (i, 0)),
            pl.BlockSpec((None, N_EXPERTS, tm), lambda i: (i // nt, 0, i % nt)),
        ],
        out_shape=[
            jax.ShapeDtypeStruct((T, D), F32),
            jax.ShapeDtypeStruct((T, D), BF16),
            jax.ShapeDtypeStruct((B, N_EXPERTS, S), F32),
        ],
        compiler_params=_params(vmem, ("arbitrary",)),
        name="out_proj",
    )(ma, mc, x2, w_all, g_all, wr_all)


ROUTE_BLK = 256


def _excl_cumsum_rows(mask_f32, tri):
    S = mask_f32.shape[0]
    carry = jnp.zeros((1, mask_f32.shape[1]), F32)
    out = []
    for i in range(S // ROUTE_BLK):
        blk = mask_f32[i * ROUTE_BLK:(i + 1) * ROUTE_BLK]
        out.append(jnp.dot(tri, blk.astype(BF16), preferred_element_type=F32) + carry)
        carry = carry + jnp.sum(blk, axis=0, keepdims=True)
    return jnp.concatenate(out, axis=0)


def _route_kernel(aff_ref, pos_tok_ref, pos_rows_ref, *, cap):
    B, E, S = aff_ref.shape
    N = B * E
    aff = aff_ref[...].reshape(N, S).T
    bits = pltpu.bitcast(aff, jnp.int32)

    def bisect(_, lh):
        lo, hi = lh
        mid = lo + ((hi - lo + 1) >> 1)
        cnt = jnp.sum((bits >= mid).astype(jnp.int32), axis=0, keepdims=True)
        ok = cnt >= cap
        return jnp.where(ok, mid, lo), jnp.where(ok, hi, mid - 1)

    lo0 = jnp.zeros((1, N), jnp.int32)
    hi0 = jnp.full((1, N), 0x7F800000, jnp.int32)
    thr, _ = lax.fori_loop(0, 31, bisect, (lo0, hi0))

    r = lax.broadcasted_iota(jnp.int32, (ROUTE_BLK, ROUTE_BLK), 0)
    c = lax.broadcasted_iota(jnp.int32, (ROUTE_BLK, ROUTE_BLK), 1)
    tri = jnp.where(r > c, 1.0, 0.0).astype(BF16)

    gt = jnp.where(bits > thr, 1.0, 0.0)
    eq = jnp.where(bits == thr, 1.0, 0.0)
    need = cap - jnp.sum(gt, axis=0, keepdims=True)
    sel = gt + eq * jnp.where(_excl_cumsum_rows(eq, tri) < need, 1.0, 0.0)
    pos = jnp.where(sel > 0.5, _excl_cumsum_rows(sel, tri), -1.0)
    for b in range(B):
        pos_tok_ref[b] = pos[:, b * E:(b + 1) * E].astype(jnp.int32)
    pos_rows_ref[...] = pos.T.astype(jnp.int32)


def _route(aff_t, *, cap):
    B, E, S = aff_t.shape
    return pl.pallas_call(
        functools.partial(_route_kernel, cap=cap),
        out_shape=[jax.ShapeDtypeStruct((B, S, E), jnp.int32),
                   jax.ShapeDtypeStruct((B * E, S), jnp.int32)],
        compiler_params=_params(32 * MIB),
        name="route",
    )(aff_t)


GATHER_EG = 8


def _gather_kernel(pos_ref, aff_ref, h_ref, xs_ref, gate_ref, p_ref, *, cap):
    S = h_ref.shape[0]
    slot = lax.broadcasted_iota(jnp.int32, (cap, S), 0)
    for e in range(GATHER_EG):
        hit = slot == pos_ref[e:e + 1, :]
        p_ref[e * cap:(e + 1) * cap, :] = jnp.where(hit, 1.0, 0.0).astype(BF16)
        g = jnp.sum(jnp.where(hit, aff_ref[0, e:e + 1, :], 0.0), axis=-1, keepdims=True)
        gate_ref[0, e] = jnp.broadcast_to(g, (cap, LANES))
    xs = jnp.dot(p_ref[...], h_ref[...], preferred_element_type=F32)
    xs_ref[0] = xs.astype(BF16).reshape(GATHER_EG, cap, -1)


def _gather(pos_rows, aff_t, h2, *, B, S, E, cap):
    D = h2.shape[1]
    ng = E // GATHER_EG
    vmem = 2 * (2 * GATHER_EG * S * 4 + S * D * 2 + GATHER_EG * cap * (D * 2 + LANES * 4)) \
        + GATHER_EG * cap * S * 2 + GATHER_EG * cap * (S + D) * 4
    return pl.pallas_call(
        functools.partial(_gather_kernel, cap=cap),
        grid=(B, ng),
        in_specs=[
            pl.BlockSpec((GATHER_EG, S), lambda b, j: (b * ng + j, 0)),
            pl.BlockSpec((1, GATHER_EG, S), lambda b, j: (b, j, 0)),
            pl.BlockSpec((S, D), lambda b, j: (b, 0)),
        ],
        out_specs=[
            pl.BlockSpec((1, GATHER_EG, cap, D), lambda b, j: (b, j, 0, 0)),
            pl.BlockSpec((1, GATHER_EG, cap, LANES), lambda b, j: (b, j, 0, 0)),
        ],
        out_shape=[
            jax.ShapeDtypeStruct((B, E, cap, D), BF16),
            jax.ShapeDtypeStruct((B, E, cap, LANES), F32),
        ],
        scratch_shapes=[pltpu.VMEM((GATHER_EG * cap, S), BF16)],
        compiler_params=_params(vmem, ("arbitrary", "arbitrary")),
        name="gather",
    )(pos_rows, aff_t, h2)


FFN_FC = 512
FFN_MC = 512


def _ffn_kernel(xs_ref, gate_ref, wg_ref, wu_ref, wd_ref, y_ref, acc_ref):
    f = pl.program_id(1)
    nB, _, cap, D = xs_ref.shape
    per = FFN_MC // cap
    nm = nB // per

    @pl.when(f == 0)
    def _():
        acc_ref[...] = jnp.zeros_like(acc_ref)

    wg = wg_ref[...].astype(BF16)
    wu = wu_ref[...].astype(BF16)
    wd = wd_ref[...].astype(BF16)

    def rows(m):
        return xs_ref[m * per:(m + 1) * per, 0].reshape(FFN_MC, D)

    def gate_up(m):
        x = rows(m)
        return (jnp.dot(x, wg, preferred_element_type=F32),
                jnp.dot(x, wu, preferred_element_type=F32))

    gu = gate_up(0)
    for m in range(nm):
        nxt = gate_up(m + 1) if m + 1 < nm else None
        g, u = gu
        hid = (g * jax.nn.sigmoid(g) * u).astype(BF16)
        acc_ref[m * FFN_MC:(m + 1) * FFN_MC, :] += jnp.dot(hid, wd, preferred_element_type=F32)
        gu = nxt

    @pl.when(f == pl.num_programs(1) - 1)
    def _():
        for m in range(nm):
            gate = gate_ref[m * per:(m + 1) * per, 0].reshape(FFN_MC, LANES)
            y = acc_ref[m * FFN_MC:(m + 1) * FFN_MC, :] * jnp.tile(gate, (1, D // LANES))
            y_ref[m * per:(m + 1) * per, 0] = y.astype(BF16).reshape(per, cap, D)


def _ffn(xs, gates, wg_all, wu_all, wd_all, l):
    B, E, cap, D = xs.shape
    FF = wg_all.shape[3]
    M = B * cap
    vmem = 4 * M * D * 2 + 2 * M * LANES * 4 + 6 * D * FFN_FC * 4 + 3 * D * FFN_FC * 2 + M * D * 4 \
        + 8 * FFN_MC * FFN_FC * 4 + 2 * FFN_MC * D * 4
    return pl.pallas_call(
        _ffn_kernel,
        grid=(E, FF // FFN_FC),
        in_specs=[
            pl.BlockSpec((B, 1, cap, D), lambda e, f: (0, e, 0, 0)),
            pl.BlockSpec((B, 1, cap, LANES), lambda e, f: (0, e, 0, 0)),
            pl.BlockSpec((None, None, D, FFN_FC), lambda e, f: (l, e, 0, f)),
            pl.BlockSpec((None, None, D, FFN_FC), lambda e, f: (l, e, 0, f)),
            pl.BlockSpec((None, None, FFN_FC, D), lambda e, f: (l, e, f, 0)),
        ],
        out_specs=pl.BlockSpec((B, 1, cap, D), lambda e, f: (0, e, 0, 0)),
        out_shape=jax.ShapeDtypeStruct((B, E, cap, D), BF16),
        scratch_shapes=[pltpu.VMEM((M, D), F32)],
        compiler_params=_params(vmem, ("arbitrary", "arbitrary")),
        name="ffn",
    )(xs, gates, wg_all, wu_all, wd_all)


COMBINE_TT = 512


def _combine_kernel(pos_ref, y_ref, x_ref, fg_ref, o_ref, p_ref, *, cap, final):
    E = pos_ref.shape[2]
    slot = lax.broadcasted_iota(jnp.int32, (COMBINE_TT, cap), 1)
    pos = pos_ref[0]
    for e in range(E):
        hit = slot == pos[:, e:e + 1]
        p_ref[:, e * cap:(e + 1) * cap] = jnp.where(hit, 1.0, 0.0).astype(BF16)
    x2 = x_ref[...] + jnp.dot(p_ref[...], y_ref[0], preferred_element_type=F32)
    o_ref[...] = _rms(x2, fg_ref[...]) if final else x2


def _combine(pos_tok, y, x1, final_g, *, B, S, cap, final):
    E = pos_tok.shape[2]
    D = x1.shape[1]
    nt = S // COMBINE_TT
    vmem = 2 * (COMBINE_TT * LANES * 4 + E * cap * D * 2 + 2 * COMBINE_TT * D * 4) \
        + COMBINE_TT * E * cap * 2 + 2 * COMBINE_TT * D * 4 + 2 * COMBINE_TT * cap * 4
    return pl.pallas_call(
        functools.partial(_combine_kernel, cap=cap, final=final),
        grid=(B, nt),
        in_specs=[
            pl.BlockSpec((1, COMBINE_TT, E), lambda b, t: (b, t, 0)),
            pl.BlockSpec((1, E * cap, D), lambda b, t: (b, 0, 0)),
            pl.BlockSpec((COMBINE_TT, D), lambda b, t: (b * nt + t, 0)),
            pl.BlockSpec((1, D), lambda b, t: (0, 0)),
        ],
        out_specs=pl.BlockSpec((COMBINE_TT, D), lambda b, t: (b * nt + t, 0)),
        out_shape=jax.ShapeDtypeStruct((B * S, D), F32),
        scratch_shapes=[pltpu.VMEM((COMBINE_TT, E * cap), BF16)],
        compiler_params=_params(vmem, ("arbitrary", "arbitrary")),
        name="combine",
    )(pos_tok, y.reshape(B, E * cap, D), x1, final_g.reshape(1, D))


def kernel(x, norm1_g, w_in, rpb, conv_w, conv_b, conv_ln_g, conv_ln_b, attn_out_g,
           conv_out_g, w_out, norm2_g, w_router, w_gate, w_up, w_down, final_g):
    B, S, D = x.shape
    depth = w_in.shape[0]
    E = w_router.shape[2]
    cap = CAP_FACTOR * S // E
    stack = lambda a: a.reshape(depth, 1, a.shape[-1])
    w_in_bf = w_in.astype(BF16)
    wr = _router_weights(w_router)
    bias = _attn_bias_tables(rpb, S // GRID_W)
    x2 = x.reshape(B * S, D)
    for l in range(depth):
        qkv, mc = _in_proj_conv(x2, stack(norm1_g), w_in_bf, conv_w, stack(conv_b), stack(conv_ln_g),
                                stack(conv_ln_b), stack(conv_out_g), l, S=S)
        ma = _attn(qkv, bias, stack(attn_out_g), l, B=B, S=S)
        x1, h2, aff = _out_proj(ma, mc, x2, w_out, stack(norm2_g), wr, l, B=B, S=S)
        pos_tok, pos_rows = _route(aff, cap=cap)
        xs, gates = _gather(pos_rows, aff, h2, B=B, S=S, E=E, cap=cap)
        y = _ffn(xs, gates, w_gate, w_up, w_down, l)
        x2 = _combine(pos_tok, y, x1, final_g, B=B, S=S, cap=cap, final=(l == depth - 1))
    return x2.reshape(B, S, D)
```

```python
import functools

import jax
import jax.numpy as jnp
import numpy as np
from jax import lax
from jax.experimental import pallas as pl
from jax.experimental.pallas import tpu as pltpu

F32 = jnp.float32
BF16 = jnp.bfloat16

GRID_W = 64
HEADS = 8
HEAD_DIM = 64
ATTN_W = HEADS * HEAD_DIM
WIN_ROWS = 8
WIN_COLS = 16
CONV_K = 31
N_EXPERTS = 16
CAP_FACTOR = 2
EPS = 1e-6
LOG2E = 1.4426950408889634

LANES = 128
SUBLANES = 8
MIB = 1024 * 1024
VMEM_CAP = 56 * MIB

NEG_BIG = -1e30


def _params(vmem_bytes, sem=None):
    kw = dict(vmem_limit_bytes=int(min(max(vmem_bytes, 16 * MIB), VMEM_CAP)))
    if sem is not None:
        kw["dimension_semantics"] = sem
    return pltpu.CompilerParams(**kw)


def _rms(x, g):
    ms = jnp.mean(x * x, axis=-1, keepdims=True)
    return x * lax.rsqrt(ms + EPS) * g


def _layer_vec(l, n):
    return pl.BlockSpec((None, 1, n), lambda *_: (l, 0, 0))


PROJ_SUB = 512


def _attn_bias_tables(rpb_all, rows):
    kh = min(WIN_ROWS, rows)
    ncol = 2 * WIN_COLS - 1
    nrow = 2 * WIN_ROWS - 1
    qc = np.arange(GRID_W)[:, None]
    kc = np.arange(GRID_W)[None, :]
    col_off = np.clip(kc - qc + (WIN_COLS - 1), 0, ncol - 1)
    col_sel = (col_off[None] == np.arange(ncol)[:, None, None]).astype(np.float32)
    var = np.arange(kh)[:, None]
    kr = np.arange(kh)[None, :]
    row_off = kr - var + (WIN_ROWS - 1)
    row_sel = (row_off[:, :, None] == np.arange(nrow)[None, None, :]).astype(np.float32)
    win_start = np.clip(qc - WIN_COLS // 2, 0, GRID_W - WIN_COLS)
    mask = (kc >= win_start) & (kc < win_start + WIN_COLS)
    L = rpb_all.shape[0]
    rows_picked = jnp.einsum("lhrc,vkr->lvhkc", rpb_all.astype(F32), jnp.asarray(row_sel),
                             precision=lax.Precision.HIGHEST)
    t = jnp.einsum("lvhkc,cqj->lvhqkj", rows_picked, jnp.asarray(col_sel),
                   precision=lax.Precision.HIGHEST)
    t = jnp.where(jnp.asarray(mask)[:, None, :], t * LOG2E, NEG_BIG)
    return t.reshape(L, kh, HEADS // 2, 2 * GRID_W, kh * GRID_W)


def _attn_kernel(qkv_ref, bias_ref, g_ref, o_ref, s0, s1, p0, p1, l0, l1, *, rows, kh):
    nk = kh * GRID_W
    npair = HEADS // 2
    lane = lax.broadcasted_iota(jnp.int32, (GRID_W, LANES), 1)
    first_head = lane < HEAD_DIM
    s_buf, p_buf, l_buf = (s0, s1), (p0, p1), (l0, l1)

    def key_row(r):
        lo, hi = 0, rows - kh
        return min(max(r - kh // 2, lo), hi) if isinstance(r, int) else jnp.clip(r - kh // 2, lo, hi)

    def row_start(r):
        return r * GRID_W if isinstance(r, int) else pl.multiple_of(r * GRID_W, GRID_W)

    def scores(r, slot):
        q0, k0 = row_start(r), row_start(key_row(r))
        for p in range(npair):
            c = p * LANES
            qp = qkv_ref[pl.ds(q0, GRID_W), c:c + LANES].astype(F32)
            qq = jnp.concatenate(
                [jnp.where(first_head, qp, 0.0), jnp.where(first_head, 0.0, qp)], axis=0
            ).astype(BF16)
            kp = qkv_ref[pl.ds(k0, nk), ATTN_W + c:ATTN_W + c + LANES]
            s_buf[slot][p] = lax.dot_general(qq, kp, (((1,), (1,)), ((), ())),
                                             preferred_element_type=F32)

    def softmax(r, slot):
        var = r - key_row(r)
        for p in range(npair):
            s = s_buf[slot][p] + bias_ref[var, p]
            e = jnp.exp2(s - jnp.max(s, axis=-1, keepdims=True))
            l_buf[slot][p] = jnp.broadcast_to(jnp.sum(e, axis=-1, keepdims=True), (2 * GRID_W, LANES))
            p_buf[slot][p] = e.astype(BF16)

    def values(r, slot):
        q0, k0 = row_start(r), row_start(key_row(r))
        outs = []
        for p in range(npair):
            c = 2 * ATTN_W + p * LANES
            vp = qkv_ref[pl.ds(k0, nk), c:c + LANES]
            pv = jnp.dot(p_buf[slot][p], vp, preferred_element_type=F32) / l_buf[slot][p]
            outs.append(jnp.where(first_head, pv[:GRID_W], pv[GRID_W:]))
        o_row = jnp.concatenate(outs, axis=-1)
        o_ref[pl.ds(q0, GRID_W), :] = _rms(o_row, g_ref[...]).astype(BF16)

    scores(0, 0)
    scores(1, 1)
    softmax(0, 0)

    def body(j, carry):
        i = 2 * j
        scores(i, 0)
        softmax(i - 1, 1)
        values(i - 2, 0)
        scores(i + 1, 1)
        softmax(i, 0)
        values(i - 1, 1)
        return carry

    lax.fori_loop(1, rows // 2, body, 0, unroll=3)
    softmax(rows - 1, 1)
    values(rows - 2, 0)
    values(rows - 1, 1)


def _attn(qkv, bias_all, g_all, l, *, B, S):
    rows = S // GRID_W
    kh = min(WIN_ROWS, rows)
    assert rows % 2 == 0 and rows >= 4, "the row pipeline alternates two buffer sets"
    tab = bias_all.shape[1:]
    nk = kh * GRID_W
    npair = HEADS // 2
    ring = 2 * npair * 2 * GRID_W * (nk * 4 + nk * 2 + LANES * 4)
    vmem = 2 * (S * 3 * ATTN_W * 2 + S * ATTN_W * 2 + int(np.prod(tab)) * 4) + ring + 8 * MIB
    buf = lambda n, dt: pltpu.VMEM((npair, 2 * GRID_W, n), dt)
    return pl.pallas_call(
        functools.partial(_attn_kernel, rows=rows, kh=kh),
        grid=(B,),
        in_specs=[
            pl.BlockSpec((S, 3 * ATTN_W), lambda b: (b, 0)),
            pl.BlockSpec((None,) + tab, lambda b: (l, 0, 0, 0, 0)),
            _layer_vec(l, ATTN_W),
        ],
        out_specs=pl.BlockSpec((S, ATTN_W), lambda b: (b, 0)),
        out_shape=jax.ShapeDtypeStruct((B * S, ATTN_W), BF16),
        scratch_shapes=[buf(nk, F32), buf(nk, F32), buf(nk, BF16), buf(nk, BF16),
                        buf(LANES, F32), buf(LANES, F32)],
        compiler_params=_params(vmem, ("arbitrary",)),
        name="attn",
    )(qkv, bias_all, g_all)


CONV_PAD = 16
CONV_TT = 64
def _in_proj_conv_kernel(x_ref, xp_ref, xn_ref, g_ref, w_ref, cw_ref, cb_ref, lg_ref, lb_ref, og_ref,
                         qkv_ref, mc_ref, h_ref, pad_ref, dw_ref, *, C, steps_per_seq):
    tm = x_ref.shape[0]
    step = pl.program_id(0) % steps_per_seq
    g = g_ref[...]
    h_ref[0:CONV_PAD, :] = _rms(xp_ref[...], g).astype(BF16)
    h_ref[CONV_PAD:CONV_PAD + tm, :] = _rms(x_ref[...], g).astype(BF16)
    h_ref[CONV_PAD + tm:, :] = _rms(xn_ref[...], g).astype(BF16)
    ag = jnp.dot(h_ref[...], w_ref[:, 3 * ATTN_W:], preferred_element_type=F32)
    u = ag[:, :C] * jax.nn.sigmoid(ag[:, C:])
    row = lax.broadcasted_iota(jnp.int32, (tm + 2 * CONV_PAD, 1), 0)
    first_in = jnp.where(step == 0, CONV_PAD, 0)
    end_in = jnp.where(step == steps_per_seq - 1, tm + CONV_PAD, tm + 2 * CONV_PAD)
    u = jnp.where((row >= first_in) & (row < end_in), u, 0.0)
    for c in range(C // LANES):
        pad_ref[c] = u[:, c * LANES:(c + 1) * LANES]

    h_main = h_ref[CONV_PAD:CONV_PAD + tm, :]
    anchors = {}
    n_ln = tm // CONV_TT
    for i, n0 in enumerate(range(0, 3 * ATTN_W, ATTN_W)):
        r = jnp.dot(h_main, w_ref[:, n0:n0 + ATTN_W], preferred_element_type=F32)
        if n0 == 0:
            r = r * (HEAD_DIM ** -0.5 * LOG2E)
        qkv_ref[:, n0:n0 + ATTN_W] = r.astype(BF16)
        bits = pltpu.bitcast(r[0:SUBLANES, 0:LANES], jnp.uint32)
        sixteen = jnp.uint32(16)
        zero = pltpu.bitcast(lax.shift_right_logical(lax.shift_right_logical(bits, sixteen), sixteen), F32)
        anchors[(i + 1) * n_ln // 4 - 1] = zero

    shift = CONV_PAD - CONV_K // 2
    nsub = CONV_TT // SUBLANES
    for c in range(C // LANES):
        sl = slice(c * LANES, (c + 1) * LANES)
        wb = [jnp.broadcast_to(cw_ref[k:k + 1, sl], (SUBLANES, LANES)) for k in range(CONV_K)]
        for t0 in range(0, tm, CONV_TT):
            accs = [jnp.zeros((SUBLANES, LANES), F32) for _ in range(nsub)]
            for off in range(SUBLANES * (nsub - 1) + CONV_K):
                r0 = t0 + shift + off
                win = pad_ref[c, r0:r0 + SUBLANES, :]
                for j in range(nsub):
                    k = off - SUBLANES * j
                    if 0 <= k < CONV_K:
                        accs[j] = accs[j] + win * wb[k]
            dw_ref[t0:t0 + CONV_TT, sl] = jnp.concatenate(accs, axis=0)
    for t0 in range(0, tm, CONV_TT):
        dw = dw_ref[t0:t0 + CONV_TT, :] + cb_ref[...]
        if t0 // CONV_TT in anchors:
            dw = dw + jnp.tile(anchors[t0 // CONV_TT], (CONV_TT // SUBLANES, C // LANES))
        mu = jnp.mean(dw, axis=-1, keepdims=True)
        d = dw - mu
        var = jnp.mean(d * d, axis=-1, keepdims=True)
        y = d * lax.rsqrt(var + EPS) * lg_ref[...] + lb_ref[...]
        z = y * jax.nn.sigmoid(y)
        mc_ref[t0:t0 + CONV_TT, :] = _rms(z, og_ref[...]).astype(BF16)


def _in_proj_conv(x2, g_all, w_all, cw_all, cb_all, lg_all, lb_all, og_all, l, *, S, tm=2 * PROJ_SUB):
    T, D = x2.shape
    N = w_all.shape[2]
    C = (N - 3 * ATTN_W) // 2
    nt = S // tm
    hb = tm // CONV_PAD
    last = T // CONV_PAD - 1
    vmem = 2 * (tm * D * 4 + tm * 3 * ATTN_W * 2 + tm * C * 2) + 2 * D * N * 2 \
        + (tm + 2 * CONV_PAD) * (D * 2 + 2 * C * 4 + 2 * C * 4) + tm * 3 * ATTN_W * 4 + tm * C * 4 + 4 * MIB
    vec = _layer_vec(l, C)
    return pl.pallas_call(
        functools.partial(_in_proj_conv_kernel, C=C, steps_per_seq=nt),
        grid=(T // tm,),
        in_specs=[
            pl.BlockSpec((tm, D), lambda i: (i, 0)),
            pl.BlockSpec((CONV_PAD, D), lambda i: (jnp.maximum(i * hb - 1, 0), 0)),
            pl.BlockSpec((CONV_PAD, D), lambda i: (jnp.minimum((i + 1) * hb, last), 0)),
            _layer_vec(l, D),
            pl.BlockSpec((None, D, N), lambda i: (l, 0, 0)),
            pl.BlockSpec((None, CONV_K, C), lambda i: (l, 0, 0)),
            vec, vec, vec, vec,
        ],
        out_specs=[
            pl.BlockSpec((tm, 3 * ATTN_W), lambda i: (i, 0)),
            pl.BlockSpec((tm, C), lambda i: (i, 0)),
        ],
        out_shape=[
            jax.ShapeDtypeStruct((T, 3 * ATTN_W), BF16),
            jax.ShapeDtypeStruct((T, C), BF16),
        ],
        scratch_shapes=[pltpu.VMEM((tm + 2 * CONV_PAD, D), BF16),
                        pltpu.VMEM((C // LANES, tm + 2 * CONV_PAD, LANES), F32),
                        pltpu.VMEM((tm, C), F32)],
        compiler_params=_params(vmem, ("arbitrary",)),
        name="in_proj_conv",
    )(x2, x2, x2, g_all, w_all, cw_all, cb_all, lg_all, lb_all, og_all)


def _router_weights(w_router_all):
    wt = jnp.swapaxes(w_router_all, 1, 2)
    hi = wt.astype(BF16)
    lo = (wt - hi.astype(F32)).astype(BF16)
    return jnp.concatenate([hi, lo], axis=1)


def _router_affinity_t(h_bf16, wr_t):
    lt = lax.dot_general(wr_t, h_bf16, (((1,), (1,)), ((), ())), preferred_element_type=F32)
    logits = lt[:N_EXPERTS] + lt[N_EXPERTS:]
    e = jnp.exp(logits - jnp.max(logits, axis=0, keepdims=True))
    return e / jnp.sum(e, axis=0, keepdims=True)


def _out_proj_kernel(ma_ref, mc_ref, x_ref, w_ref, g_ref, wr_ref, x1_ref, h2_ref, aff_ref):
    wa = ma_ref.shape[1]
    subs = [slice(i, i + PROJ_SUB) for i in range(0, x_ref.shape[0], PROJ_SUB)]
    w_attn = w_ref[:wa].astype(BF16)
    w_conv = w_ref[wa:].astype(BF16)
    accs = [jnp.dot(ma_ref[r, :], w_attn, preferred_element_type=F32)
            + jnp.dot(mc_ref[r, :], w_conv, preferred_element_type=F32) for r in subs]
    for r, acc in zip(subs, accs):
        x1 = x_ref[r, :] + acc
        x1_ref[r, :] = x1
        h = _rms(x1, g_ref[...]).astype(BF16)
        h2_ref[r, :] = h
        aff_ref[:, r] = _router_affinity_t(h, wr_ref[...])


def _out_proj(ma, mc, x2, w_all, g_all, wr_all, l, *, B, S, tm=2 * PROJ_SUB):
    T, D = x2.shape
    Wa = ma.shape[1]
    Wc = mc.shape[1]
    nt = S // tm
    vmem = 2 * (tm * (Wa + Wc) * 2 + 2 * tm * D * 4 + tm * D * 2 + N_EXPERTS * tm * 4) \
        + 2 * (D * D * 4 + 2 * N_EXPERTS * D * 2) + D * D * 2 + 2 * tm * D * 4 + tm * D * 2 + 4 * MIB
    return pl.pallas_call(
        _out_proj_kernel,
        grid=(T // tm,),
        in_specs=[
            pl.BlockSpec((tm, Wa), lambda i: (i, 0)),
            pl.BlockSpec((tm, Wc), lambda i: (i, 0)),
            pl.BlockSpec((tm, D), lambda i: (i, 0)),
            pl.BlockSpec((None, Wa + Wc, D), lambda i: (l, 0, 0)),
            _layer_vec(l, D),
            pl.BlockSpec((None, 2 * N_EXPERTS, D), lambda i: (l, 0, 0)),
        ],
        out_specs=[
            pl.BlockSpec((tm, D), lambda i: (i, 0)),
            pl.BlockSpec((tm, D), lambda i: (i, 0)),
            pl.BlockSpec((None, N_EXPERTS, tm), lambda i: (i // nt, 0, i % nt)),
        ],
        out_shape=[
            jax.ShapeDtypeStruct((T, D), F32),
            jax.ShapeDtypeStruct((T, D), BF16),
            jax.ShapeDtypeStruct((B, N_EXPERTS, S), F32),
        ],
        compiler_params=_params(vmem, ("arbitrary",)),
        name="out_proj",
    )(ma, mc, x2, w_all, g_all, wr_all)


ROUTE_BLK = 256


def _excl_cumsum_rows(mask_f32, tri):
    S = mask_f32.shape[0]
    carry = jnp.zeros((1, mask_f32.shape[1]), F32)
    out = []
    for i in range(S // ROUTE_BLK):
        blk = mask_f32[i * ROUTE_BLK:(i + 1) * ROUTE_BLK]
        out.append(jnp.dot(tri, blk.astype(BF16), preferred_element_type=F32) + carry)
        carry = carry + jnp.sum(blk, axis=0, keepdims=True)
    return jnp.concatenate(out, axis=0)


def _route_kernel(aff_ref, pos_tok_ref, pos_rows_ref, *, cap):
    B, E, S = aff_ref.shape
    N = B * E
    aff = aff_ref[...].reshape(N, S).T
    bits = pltpu.bitcast(aff, jnp.int32)

    def bisect(_, lh):
        lo, hi = lh
        mid = lo + ((hi - lo + 1) >> 1)
        cnt = jnp.sum((bits >= mid).astype(jnp.int32), axis=0, keepdims=True)
        ok = cnt >= cap
        return jnp.where(ok, mid, lo), jnp.where(ok, hi, mid - 1)

    lo0 = jnp.zeros((1, N), jnp.int32)
    hi0 = jnp.full((1, N), 0x7F800000, jnp.int32)
    thr, _ = lax.fori_loop(0, 31, bisect, (lo0, hi0))

    r = lax.broadcasted_iota(jnp.int32, (ROUTE_BLK, ROUTE_BLK), 0)
    c = lax.broadcasted_iota(jnp.int32, (ROUTE_BLK, ROUTE_BLK), 1)
    tri = jnp.where(r > c, 1.0, 0.0).astype(BF16)

    gt = jnp.where(bits > thr, 1.0, 0.0)
    eq = jnp.where(bits == thr, 1.0, 0.0)
    need = cap - jnp.sum(gt, axis=0, keepdims=True)
    sel = gt + eq * jnp.where(_excl_cumsum_rows(eq, tri) < need, 1.0, 0.0)
    pos = jnp.where(sel > 0.5, _excl_cumsum_rows(sel, tri), -1.0)
    for b in range(B):
        pos_tok_ref[b] = pos[:, b * E:(b + 1) * E].astype(jnp.int32)
    pos_rows_ref[...] = pos.T.astype(jnp.int32)


def _route(aff_t, *, cap):
    B, E, S = aff_t.shape
    return pl.pallas_call(
        functools.partial(_route_kernel, cap=cap),
        out_shape=[jax.ShapeDtypeStruct((B, S, E), jnp.int32),
                   jax.ShapeDtypeStruct((B * E, S), jnp.int32)],
        compiler_params=_params(32 * MIB),
        name="route",
    )(aff_t)


GATHER_EG = 8


def _gather_kernel(pos_ref, aff_ref, h_ref, xs_ref, gate_ref, p_ref, *, cap):
    S = h_ref.shape[0]
    slot = lax.broadcasted_iota(jnp.int32, (cap, S), 0)
    for e in range(GATHER_EG):
        hit = slot == pos_ref[e:e + 1, :]
        p_ref[e * cap:(e + 1) * cap, :] = jnp.where(hit, 1.0, 0.0).astype(BF16)
        g = jnp.sum(jnp.where(hit, aff_ref[0, e:e + 1, :], 0.0), axis=-1, keepdims=True)
        gate_ref[0, e] = jnp.broadcast_to(g, (cap, LANES))
    xs = jnp.dot(p_ref[...], h_ref[...], preferred_element_type=F32)
    xs_ref[0] = xs.astype(BF16).reshape(GATHER_EG, cap, -1)


def _gather(pos_rows, aff_t, h2, *, B, S, E, cap):
    D = h2.shape[1]
    ng = E // GATHER_EG
    vmem = 2 * (2 * GATHER_EG * S * 4 + S * D * 2 + GATHER_EG * cap * (D * 2 + LANES * 4)) \
        + GATHER_EG * cap * S * 2 + GATHER_EG * cap * (S + D) * 4
    return pl.pallas_call(
        functools.partial(_gather_kernel, cap=cap),
        grid=(B, ng),
        in_specs=[
            pl.BlockSpec((GATHER_EG, S), lambda b, j: (b * ng + j, 0)),
            pl.BlockSpec((1, GATHER_EG, S), lambda b, j: (b, j, 0)),
            pl.BlockSpec((S, D), lambda b, j: (b, 0)),
        ],
        out_specs=[
            pl.BlockSpec((1, GATHER_EG, cap, D), lambda b, j: (b, j, 0, 0)),
            pl.BlockSpec((1, GATHER_EG, cap, LANES), lambda b, j: (b, j, 0, 0)),
        ],
        out_shape=[
            jax.ShapeDtypeStruct((B, E, cap, D), BF16),
            jax.ShapeDtypeStruct((B, E, cap, LANES), F32),
        ],
        scratch_shapes=[pltpu.VMEM((GATHER_EG * cap, S), BF16)],
        compiler_params=_params(vmem, ("arbitrary", "arbitrary")),
        name="gather",
    )(pos_rows, aff_t, h2)


FFN_FC = 512
FFN_MC = 512


def _ffn_kernel(xs_ref, gate_ref, wg_ref, wu_ref, wd_ref, y_ref, acc_ref):
    f = pl.program_id(1)
    nB, _, cap, D = xs_ref.shape
    per = FFN_MC // cap
    nm = nB // per

    @pl.when(f == 0)
    def _():
        acc_ref[...] = jnp.zeros_like(acc_ref)

    wg = wg_ref[...].astype(BF16)
    wu = wu_ref[...].astype(BF16)
    wd = wd_ref[...].astype(BF16)

    def rows(m):
        return xs_ref[m * per:(m + 1) * per, 0].reshape(FFN_MC, D)

    def gate_up(m):
        x = rows(m)
        return (jnp.dot(x, wg, preferred_element_type=F32),
                jnp.dot(x, wu, preferred_element_type=F32))

    gu = gate_up(0)
    for m in range(nm):
        nxt = gate_up(m + 1) if m + 1 < nm else None
        g, u = gu
        hid = (g * jax.nn.sigmoid(g) * u).astype(BF16)
        acc_ref[m * FFN_MC:(m + 1) * FFN_MC, :] += jnp.dot(hid, wd, preferred_element_type=F32)
        gu = nxt

    @pl.when(f == pl.num_programs(1) - 1)
    def _():
        for m in range(nm):
            gate = gate_ref[m * per:(m + 1) * per, 0].reshape(FFN_MC, LANES)
            y = acc_ref[m * FFN_MC:(m + 1) * FFN_MC, :] * jnp.tile(gate, (1, D // LANES))
            y_ref[m * per:(m + 1) * per, 0] = y.astype(BF16).reshape(per, cap, D)


def _ffn(xs, gates, wg_all, wu_all, wd_all, l):
    B, E, cap, D = xs.shape
    FF = wg_all.shape[3]
    M = B * cap
    vmem = 4 * M * D * 2 + 2 * M * LANES * 4 + 6 * D * FFN_FC * 4 + 3 * D * FFN_FC * 2 + M * D * 4 \
        + 8 * FFN_MC * FFN_FC * 4 + 2 * FFN_MC * D * 4
    return pl.pallas_call(
        _ffn_kernel,
        grid=(E, FF // FFN_FC),
        in_specs=[
            pl.BlockSpec((B, 1, cap, D), lambda e, f: (0, e, 0, 0)),
            pl.BlockSpec((B, 1, cap, LANES), lambda e, f: (0, e, 0, 0)),
            pl.BlockSpec((None, None, D, FFN_FC), lambda e, f: (l, e, 0, f)),
            pl.BlockSpec((None, None, D, FFN_FC), lambda e, f: (l, e, 0, f)),
            pl.BlockSpec((None, None, FFN_FC, D), lambda e, f: (l, e, f, 0)),
        ],
        out_specs=pl.BlockSpec((B, 1, cap, D), lambda e, f: (0, e, 0, 0)),
        out_shape=jax.ShapeDtypeStruct((B, E, cap, D), BF16),
        scratch_shapes=[pltpu.VMEM((M, D), F32)],
        compiler_params=_params(vmem, ("arbitrary", "arbitrary")),
        name="ffn",
    )(xs, gates, wg_all, wu_all, wd_all)


COMBINE_TT = 1024


def _combine_kernel(pos_ref, y_ref, x_ref, fg_ref, o_ref, p_ref, *, cap, final):
    E = pos_ref.shape[2]
    slot = lax.broadcasted_iota(jnp.int32, (COMBINE_TT, cap), 1)
    pos = pos_ref[0]
    for e in range(E):
        hit = slot == pos[:, e:e + 1]
        p_ref[:, e * cap:(e + 1) * cap] = jnp.where(hit, 1.0, 0.0).astype(BF16)
    x2 = x_ref[...] + jnp.dot(p_ref[...], y_ref[0], preferred_element_type=F32)
    o_ref[...] = _rms(x2, fg_ref[...]) if final else x2


def _combine(pos_tok, y, x1, final_g, *, B, S, cap, final):
    E = pos_tok.shape[2]
    D = x1.shape[1]
    nt = S // COMBINE_TT
    vmem = 2 * (COMBINE_TT * LANES * 4 + E * cap * D * 2 + 2 * COMBINE_TT * D * 4) \
        + COMBINE_TT * E * cap * 2 + 3 * COMBINE_TT * D * 4 + 2 * COMBINE_TT * cap * 4
    return pl.pallas_call(
        functools.partial(_combine_kernel, cap=cap, final=final),
        grid=(B, nt),
        in_specs=[
            pl.BlockSpec((1, COMBINE_TT, E), lambda b, t: (b, t, 0)),
            pl.BlockSpec((1, E * cap, D), lambda b, t: (b, 0, 0)),
            pl.BlockSpec((COMBINE_TT, D), lambda b, t: (b * nt + t, 0)),
            pl.BlockSpec((1, D), lambda b, t: (0, 0)),
        ],
        out_specs=pl.BlockSpec((COMBINE_TT, D), lambda b, t: (b * nt + t, 0)),
        out_shape=jax.ShapeDtypeStruct((B * S, D), F32),
        scratch_shapes=[pltpu.VMEM((COMBINE_TT, E * cap), BF16)],
        compiler_params=_params(vmem, ("arbitrary", "arbitrary")),
        name="combine",
    )(pos_tok, y.reshape(B, E * cap, D), x1, final_g.reshape(1, D))


def kernel(x, norm1_g, w_in, rpb, conv_w, conv_b, conv_ln_g, conv_ln_b, attn_out_g,
           conv_out_g, w_out, norm2_g, w_router, w_gate, w_up, w_down, final_g):
    B, S, D = x.shape
    depth = w_in.shape[0]
    E = w_router.shape[2]
    cap = CAP_FACTOR * S // E
    stack = lambda a: a.reshape(depth, 1, a.shape[-1])
    w_in_bf = w_in.astype(BF16)
    wr = _router_weights(w_router)
    bias = _attn_bias_tables(rpb, S // GRID_W)
    x2 = x.reshape(B * S, D)
    for l in range(depth):
        qkv, mc = _in_proj_conv(x2, stack(norm1_g), w_in_bf, conv_w, stack(conv_b), stack(conv_ln_g),
                                stack(conv_ln_b), stack(conv_out_g), l, S=S)
        ma = _attn(qkv, bias, stack(attn_out_g), l, B=B, S=S)
        x1, h2, aff = _out_proj(ma, mc, x2, w_out, stack(norm2_g), wr, l, B=B, S=S)
        pos_tok, pos_rows = _route(aff, cap=cap)
        xs, gates = _gather(pos_rows, aff, h2, B=B, S=S, E=E, cap=cap)
        y = _ffn(xs, gates, w_gate, w_up, w_down, l)
        x2 = _combine(pos_tok, y, x1, final_g, B=B, S=S, cap=cap, final=(l == depth - 1))
    return x2.reshape(B, S, D)
```
